```python
import jax, jax.numpy as jnp
from jax import lax
import numpy as np

D_MODEL = 1024
BATCH = 8
SEQ = 8192
DEPTH = 1
DEC_BATCH = 128
DEC_SEQ = 4
PAST_LEN = 8192
PAGE_SIZE = 128

D_HEAD = 64
H_RET = 8
H_SB = 8
D_RET = H_RET * D_HEAD
D_SB = H_SB * D_HEAD
D_MIX = D_RET + D_SB
D_IN = 4 * D_RET + 3 * D_SB
D_FF = ((8 * D_MODEL + 3 * 256 - 1) // (3 * 256)) * 256
RET_CHUNK = 128
SB_BLOCK = 128
SB_BIAS_INIT = -8.0
ROPE_BASE = 10000.0
EPS = 1e-6

kernel_name = "hymba_retention_stickbreaking_step"


def rmsnorm(x, w):
    xf = x.astype(jnp.float32)
    y = xf * lax.rsqrt(jnp.mean(xf * xf, axis=-1, keepdims=True) + EPS)
    return (y * w.astype(jnp.float32)).astype(x.dtype)


def log_gamma():
    h = jnp.arange(H_RET, dtype=jnp.float32)
    return jnp.log1p(-jnp.exp2(-5.0 - h))


def rope(x, pos):
    half = x.shape[-1] // 2
    inv = ROPE_BASE ** (-(jnp.arange(half, dtype=jnp.float32) / half))
    ang = pos.astype(jnp.float32)[:, None] * inv[None, :]
    cos = jnp.cos(ang)[None, :, None, :]
    sin = jnp.sin(ang)[None, :, None, :]
    x1, x2 = x[..., :half], x[..., half:]
    return jnp.concatenate([x1 * cos - x2 * sin, x1 * sin + x2 * cos], axis=-1)


def mixer_inputs(x, norm_w, w_in, pos):
    B, S, _ = x.shape
    h = rmsnorm(x, norm_w)
    z = (h @ w_in).astype(jnp.float32)
    o = np.cumsum([D_RET, D_RET, D_RET, D_RET, D_SB, D_SB])
    rq, rk, rv, rg, sq, sk, sv = jnp.split(z, o, axis=-1)
    hd = lambda t, H: t.reshape(B, S, H, D_HEAD)
    rq = rope(hd(rq, H_RET), pos)
    rk = rope(hd(rk, H_RET), pos) * (D_HEAD ** -0.5)
    return rq, rk, hd(rv, H_RET), rg, hd(sq, H_SB), hd(sk, H_SB), hd(sv, H_SB)


def retention_chunk(q, k, v, state, lg):
    C = q.shape[2]
    idx = jnp.arange(C, dtype=jnp.float32)
    diff = idx[:, None] - idx[None, :]
    decay = jnp.where(diff >= 0, jnp.exp(lg[:, None, None] * jnp.maximum(diff, 0.0)), 0.0)
    qk = jnp.einsum('bhid,bhjd->bhij', q, k) * decay[None]
    o = jnp.einsum('bhij,bhje->bhie', qk, v)
    cross = jnp.exp(lg[:, None] * (idx[None, :] + 1.0))
    o = o + jnp.einsum('bhid,bhde->bhie', q, state) * cross[None, :, :, None]
    kdec = k * jnp.exp(lg[:, None] * (C - 1.0 - idx[None, :]))[None, :, :, None]
    new_state = jnp.exp(lg * C)[None, :, None, None] * state + jnp.einsum('bhjd,bhje->bhde', kdec, v)
    return o, new_state


def retention_prompt(q, k, v, lg):
    B, S, H, D = q.shape
    nc = S // RET_CHUNK
    to_chunks = lambda t: t.reshape(B, nc, RET_CHUNK, H, D).transpose(1, 0, 3, 2, 4)
    s0 = jnp.zeros((B, H, D, D), jnp.float32)

    def step(s, qkv):
        qc, kc, vc = qkv
        o, s = retention_chunk(qc, kc, vc, s, lg)
        return s, o

    s_fin, o = lax.scan(step, s0, (to_chunks(q), to_chunks(k), to_chunks(v)))
    o = o.transpose(1, 0, 3, 2, 4).reshape(B, S, H, D)
    return o, s_fin


def stick_breaking(z, mask):
    log_beta = jax.nn.log_sigmoid(z)
    log_rest = jnp.where(mask, jax.nn.log_sigmoid(-z), 0.0)
    suffix = lax.cumsum(log_rest, axis=z.ndim - 1, reverse=True) - log_rest
    return jnp.where(mask, jnp.exp(log_beta + suffix), 0.0)


def sb_prompt(q, k, v, bias):
    B, S, H, D = q.shape
    nb = S // SB_BLOCK
    scale = D ** -0.5
    qb = q.reshape(B, nb, SB_BLOCK, H, D).transpose(1, 0, 2, 3, 4)
    kpos = jnp.arange(S)
    b = bias.astype(jnp.float32)[None, :, None, None]

    def block(args):
        i, qi = args
        z = jnp.einsum('bqhd,bkhd->bhqk', qi, k) * scale + b
        qpos = i * SB_BLOCK + jnp.arange(SB_BLOCK)
        a = stick_breaking(z, kpos[None, :] < qpos[:, None])
        return jnp.einsum('bhqk,bkhd->bqhd', a, v)

    o = lax.map(block, (jnp.arange(nb), qb))
    return o.transpose(1, 0, 2, 3, 4).reshape(B, S, H, D)


def sb_sample(q, k_past, v_past, k_new, v_new, bias):
    T, D = q.shape[1], q.shape[-1]
    P = k_past.shape[1]
    scale = D ** -0.5
    z = jnp.concatenate([jnp.einsum('bqhd,bkhd->bhqk', q, k_past),
                         jnp.einsum('bqhd,bkhd->bhqk', q, k_new)], axis=-1) * scale
    z = z + bias.astype(jnp.float32)[None, :, None, None]
    kpos = jnp.arange(P + T)
    qpos = P + jnp.arange(T)
    a = stick_breaking(z, kpos[None, :] < qpos[:, None])
    return (jnp.einsum('bhqk,bkhd->bqhd', a[..., :P], v_past)
            + jnp.einsum('bhqk,bkhd->bqhd', a[..., P:], v_new))


def merge_heads(o_ret, g, o_sb, ret_norm_w, w_out, dtype):
    B, S = o_ret.shape[:2]
    mu = jnp.mean(o_ret, axis=-1, keepdims=True)
    var = jnp.mean(jnp.square(o_ret - mu), axis=-1, keepdims=True)
    n = ((o_ret - mu) * lax.rsqrt(var + EPS)).reshape(B, S, D_RET) * ret_norm_w.astype(jnp.float32)
    r = jax.nn.silu(g) * n
    cat = jnp.concatenate([r, o_sb.reshape(B, S, D_SB)], axis=-1).astype(dtype)
    return cat @ w_out


def swiglu(x, norm_w, w_gate, w_up, w_down):
    h = rmsnorm(x, norm_w)
    return (jax.nn.silu(h @ w_gate) * (h @ w_up)) @ w_down


def setup_inputs(seed: int = 0) -> dict:
    key = jax.random.key(seed)
    ks = jax.random.split(key, 17)
    n_pages = PAST_LEN // PAGE_SIZE
    n_phys = (5 * DEC_BATCH * n_pages) // 4
    f32 = jnp.float32
    nrm = lambda k, shape, s=1.0: jax.random.normal(k, shape, f32) * s
    page_table = jax.random.permutation(ks[0], n_phys)[:DEC_BATCH * n_pages].reshape(DEC_BATCH, n_pages).astype(jnp.int32)
    return {
        "x_prompt": nrm(ks[1], (BATCH, SEQ, D_MODEL)),
        "x_sample": nrm(ks[2], (DEC_BATCH, DEC_SEQ, D_MODEL)),
        "cache_k": nrm(ks[3], (DEPTH, n_phys, PAGE_SIZE, H_SB, D_HEAD)),
        "cache_v": nrm(ks[4], (DEPTH, n_phys, PAGE_SIZE, H_SB, D_HEAD)),
        "state_ret": nrm(ks[5], (DEPTH, DEC_BATCH, H_RET, D_HEAD, D_HEAD)),
        "page_table": page_table,
        "norm1_w": 1.0 + nrm(ks[6], (DEPTH, D_MODEL), 0.02),
        "w_in": nrm(ks[7], (DEPTH, D_MODEL, D_IN), D_MODEL ** -0.5),
        "sb_bias": SB_BIAS_INIT + nrm(ks[15], (DEPTH, H_SB), 0.1),
        "ret_norm_w": 1.0 + nrm(ks[8], (DEPTH, D_RET), 0.02),
        "w_out": nrm(ks[9], (DEPTH, D_MIX, D_MODEL), D_MIX ** -0.5),
        "norm2_w": 1.0 + nrm(ks[10], (DEPTH, D_MODEL), 0.02),
        "w_gate": nrm(ks[11], (DEPTH, D_MODEL, D_FF), D_MODEL ** -0.5),
        "w_up": nrm(ks[12], (DEPTH, D_MODEL, D_FF), D_MODEL ** -0.5),
        "w_down": nrm(ks[13], (DEPTH, D_FF, D_MODEL), D_FF ** -0.5),
        "final_norm_w": 1.0 + nrm(ks[14], (D_MODEL,), 0.02),
    }


def reference(x_prompt, x_sample, cache_k, cache_v, state_ret, page_table,
              norm1_w, w_in, sb_bias, ret_norm_w, w_out, norm2_w, w_gate, w_up, w_down, final_norm_w):
    lg = log_gamma()
    xp, xs = x_prompt, x_sample
    Bp, S = xp.shape[:2]
    Bs, T = xs.shape[:2]
    pos_p = jnp.arange(S)
    pos_s = PAST_LEN + jnp.arange(T)
    kp_l, vp_l, sp_l, ks_l, vs_l, ss_l = [], [], [], [], [], []
    for l in range(DEPTH):
        rq, rk, rv, rg, sq, sk, sv = mixer_inputs(xp, norm1_w[l], w_in[l], pos_p)
        o_ret, s_p = retention_prompt(rq, rk, rv, lg)
        o_sb = sb_prompt(sq, sk, sv, sb_bias[l])
        xp = xp + merge_heads(o_ret, rg, o_sb, ret_norm_w[l], w_out[l], xp.dtype)
        xp = xp + swiglu(xp, norm2_w[l], w_gate[l], w_up[l], w_down[l])
        kp_l.append(sk.astype(cache_k.dtype))
        vp_l.append(sv.astype(cache_v.dtype))
        sp_l.append(s_p.astype(state_ret.dtype))
        rq, rk, rv, rg, sq, sk, sv = mixer_inputs(xs, norm1_w[l], w_in[l], pos_s)
        tr = lambda t: t.transpose(0, 2, 1, 3)
        o_r, s_s = retention_chunk(tr(rq), tr(rk), tr(rv), state_ret[l].astype(jnp.float32), lg)
        o_ret = tr(o_r)
        k_past = cache_k[l][page_table].reshape(Bs, PAST_LEN, H_SB, D_HEAD).astype(jnp.float32)
        v_past = cache_v[l][page_table].reshape(Bs, PAST_LEN, H_SB, D_HEAD).astype(jnp.float32)
        o_sb = sb_sample(sq, k_past, v_past, sk, sv, sb_bias[l])
        xs = xs + merge_heads(o_ret, rg, o_sb, ret_norm_w[l], w_out[l], xs.dtype)
        xs = xs + swiglu(xs, norm2_w[l], w_gate[l], w_up[l], w_down[l])
        ks_l.append(sk.astype(cache_k.dtype))
        vs_l.append(sv.astype(cache_v.dtype))
        ss_l.append(s_s.astype(state_ret.dtype))
    y_prompt = rmsnorm(xp, final_norm_w)
    y_sample = rmsnorm(xs, final_norm_w)
    new_k_prompt = jnp.stack(kp_l)
    new_v_prompt = jnp.stack(vp_l)
    ret_state_prompt = jnp.stack(sp_l)
    new_k_sample = jnp.stack(ks_l)
    new_v_sample = jnp.stack(vs_l)
    ret_state_sample = jnp.stack(ss_l)
    return (y_prompt, y_sample, new_k_prompt, new_v_prompt, ret_state_prompt, new_k_sample, new_v_sample, ret_state_sample)
```

```python
import functools
import math

import jax
import jax.numpy as jnp
import numpy as np
from jax import lax
from jax.experimental import pallas as pl
from jax.experimental.pallas import tpu as pltpu

D_HEAD = 64
H_RET = 8
H_SB = 8
D_RET = H_RET * D_HEAD
D_SB = H_SB * D_HEAD
RET_CHUNK = 128
ROPE_BASE = 10000.0
EPS = 1e-6
LANES = 128
PAIRS = D_SB // LANES
VMEM_LIMIT = 56 * 1024 * 1024

F32 = jnp.float32
BF16 = jnp.bfloat16


def _dot(a, b):
    return jnp.dot(a, b, preferred_element_type=F32)


def _dot_nt(a, b):
    return lax.dot_general(a, b, (((1,), (1,)), ((), ())), preferred_element_type=F32)


def _dot_tn(a, b):
    return lax.dot_general(a, b, (((0,), (0,)), ((), ())), preferred_element_type=F32)


def _split2(x):
    hi = x.astype(BF16)
    lo = (x - hi.astype(F32)).astype(BF16)
    return hi, lo


def _split3(x):
    hi = x.astype(BF16)
    r = x - hi.astype(F32)
    mid = r.astype(BF16)
    lo = (r - mid.astype(F32)).astype(BF16)
    return hi, mid, lo


def _head_split(x2, lane_is_first):
    zero = jnp.zeros_like(x2)
    return jnp.concatenate([jnp.where(lane_is_first, x2, zero), jnp.where(lane_is_first, zero, x2)], axis=0)


def _log_sigmoid_pair(z):
    lb = jnp.minimum(z, 0.0) - jnp.log(1.0 + jnp.exp(-jnp.abs(z)))
    return lb, lb - z


def _rms_scale(x, w):
    ms = jnp.mean(x * x, axis=-1, keepdims=True)
    return x * lax.rsqrt(ms + EPS) * w


def _const_spec(shape):
    nd = len(shape)
    return pl.BlockSpec(shape, lambda *_: (0,) * nd, pipeline_mode=pl.Buffered(1))


def _in_proj_kernel(x_ref, nw_ref, w_ref, cos_ref, sin_ref,
                    rq_ref, rk_ref, rv_ref, rg_ref, sq_ref, sk_ref, sv_ref, skb_ref, svb_ref):
    h = _rms_scale(x_ref[...], nw_ref[...]).astype(BF16)
    cos = cos_ref[...]
    sin = sin_ref[...]
    first_half = (lax.broadcasted_iota(jnp.int32, (1, LANES), 1) % D_HEAD) < (D_HEAD // 2)

    def group(g):
        return _dot(h, w_ref[:, g * D_RET:(g + 1) * D_RET])

    def rope(z):
        outs = []
        for c in range(D_RET // LANES):
            zc = z[:, c * LANES:(c + 1) * LANES]
            swapped = jnp.where(first_half,
                                pltpu.roll(zc, LANES - D_HEAD // 2, axis=1),
                                pltpu.roll(zc, D_HEAD // 2, axis=1))
            outs.append(zc * cos + swapped * sin)
        return jnp.concatenate(outs, axis=1)

    scale = D_HEAD ** -0.5
    rq_ref[...] = rope(group(0)).astype(BF16)
    rk_ref[...] = rope(group(1)) * scale
    rv_ref[...] = group(2).astype(BF16)
    rg_ref[...] = group(3)
    sq_ref[...] = (group(4) * scale).astype(BF16)
    sk = group(5)
    sk_ref[...] = sk
    skb_ref[...] = sk.astype(BF16)
    sv = group(6)
    sv_ref[...] = sv
    svb_ref[...] = sv.astype(BF16)


def _rope_tables(pos):
    half = D_HEAD // 2
    inv = ROPE_BASE ** (-(jnp.arange(half, dtype=F32) / half))
    ang = pos.astype(F32)[:, None] * inv[None, :]
    cos, sin = jnp.cos(ang), jnp.sin(ang)
    reps = LANES // D_HEAD
    return (jnp.tile(jnp.concatenate([cos, cos], axis=1), (1, reps)),
            jnp.tile(jnp.concatenate([-sin, sin], axis=1), (1, reps)))


def _in_proj(x, norm_w, w_bf16, pos, tm):
    n, dm = x.shape
    cos, sin = _rope_tables(pos)
    pos_tiles = pos.shape[0] // tm
    row = lambda i: (i, 0)
    tab = lambda i: (i % pos_tiles, 0)
    f32_out = jax.ShapeDtypeStruct((n, D_RET), F32)
    bf_out = jax.ShapeDtypeStruct((n, D_RET), BF16)
    out_spec = pl.BlockSpec((tm, D_RET), row)
    return pl.pallas_call(
        _in_proj_kernel,
        grid=(n // tm,),
        in_specs=[pl.BlockSpec((tm, dm), row), _const_spec((1, dm)), _const_spec(w_bf16.shape),
                  pl.BlockSpec((tm, LANES), tab), pl.BlockSpec((tm, LANES), tab)],
        out_specs=[out_spec] * 9,
        out_shape=[bf_out, f32_out, bf_out, f32_out, bf_out, f32_out, f32_out, bf_out, bf_out],
        compiler_params=pltpu.CompilerParams(dimension_semantics=("parallel",), vmem_limit_bytes=VMEM_LIMIT),
        name="in_proj",
    )(x, norm_w.reshape(1, dm), w_bf16, cos, sin)


def _ret_tables(lg, c):
    idx = jnp.arange(c, dtype=F32)
    diff = idx[:, None] - idx[None, :]
    decay = jnp.where(diff >= 0, jnp.exp(lg[:, None, None] * jnp.maximum(diff, 0.0)), 0.0)
    cross = jnp.exp(lg[:, None] * (idx[None, :] + 1.0))
    kdec = jnp.exp(lg[:, None] * (c - 1.0 - idx[None, :]))
    sdec = jnp.exp(lg * c)
    return decay, cross, kdec, sdec


def _ret_prompt_kernel(q_ref, k_ref, v_ref, dec_ref, cross_ref, kdec_ref, sdec_ref, o_ref, s_ref, s_scr):
    c = pl.program_id(1)

    @pl.when(c == 0)
    def _():
        s_scr[...] = jnp.zeros_like(s_scr)

    lane = lax.broadcasted_iota(jnp.int32, (1, LANES), 1)
    first = lane < D_HEAD
    row_first = lax.broadcasted_iota(jnp.int32, (LANES, 1), 0) < D_HEAD
    same_head = row_first == first

    for p in range(PAIRS):
        sl = slice(p * LANES, (p + 1) * LANES)
        q2 = q_ref[:, sl]
        k2f = k_ref[:, sl]
        v2 = v_ref[:, sl]
        qk = _dot_nt(q2, _head_split(k2f.astype(BF16), first))
        prob = (qk * dec_ref[p]).astype(BF16)
        o = _dot(prob, _head_split(v2, first))
        s = s_scr[p]
        o = o + _dot(q2, s.astype(BF16)) * cross_ref[:, sl]
        o_ref[:, sl] = o
        kd = (k2f * kdec_ref[:, sl]).astype(BF16)
        upd = _dot_tn(kd, v2)
        s_scr[p] = sdec_ref[p] * s + jnp.where(same_head, upd, 0.0)

    @pl.when(c == pl.num_programs(1) - 1)
    def _():
        s_ref[...] = s_scr[...]


def _ret_prompt(rq, rk, rv, lg):
    b, s, _ = rq.shape
    c = RET_CHUNK
    decay, cross, kdec, sdec = _ret_tables(lg, c)
    dec_pairs = decay.reshape(PAIRS, 2, c, c).transpose(0, 2, 1, 3).reshape(PAIRS, c, 2 * c)
    lanes = lambda t: jnp.repeat(t.T, D_HEAD, axis=1)
    sdec_pairs = jnp.broadcast_to(jnp.repeat(sdec, D_HEAD).reshape(PAIRS, LANES, 1), (PAIRS, LANES, LANES))
    blk = pl.BlockSpec((None, c, D_RET), lambda bi, ci: (bi, ci, 0))
    o, st = pl.pallas_call(
        _ret_prompt_kernel,
        grid=(b, s // c),
        in_specs=[blk, blk, blk, _const_spec((PAIRS, c, 2 * c)), _const_spec((c, D_RET)),
                  _const_spec((c, D_RET)), _const_spec((PAIRS, LANES, LANES))],
        out_specs=[blk, pl.BlockSpec((None, PAIRS, LANES, LANES), lambda bi, ci: (bi, 0, 0, 0))],
        out_shape=[jax.ShapeDtypeStruct((b, s, D_RET), F32),
                   jax.ShapeDtypeStruct((b, PAIRS, LANES, LANES), F32)],
        scratch_shapes=[pltpu.VMEM((PAIRS, LANES, LANES), F32)],
        compiler_params=pltpu.CompilerParams(dimension_semantics=("parallel", "arbitrary"),
                                             vmem_limit_bytes=VMEM_LIMIT),
        name="ret_prompt",
    )(rq, rk, rv, dec_pairs, lanes(cross), lanes(kdec), sdec_pairs)
    st = st.reshape(b, PAIRS, 2, D_HEAD, 2, D_HEAD)
    st = jnp.stack([st[:, :, 0, :, 0, :], st[:, :, 1, :, 1, :]], axis=2).reshape(b, H_RET, D_HEAD, D_HEAD)
    return o, st


def _suffix_matrix(n):
    j = np.arange(2 * n)[:, None] % n
    s = np.arange(2 * n)[None, :]
    return jnp.asarray(np.where(s < n, j > s, True), dtype=BF16)


def _sb_prompt_kernel(bias_ref, tri_ref, q_ref, k_ref, v_ref, o_ref, acc_ref, c_ref, *, blk):
    i = pl.program_id(2)
    first = lax.broadcasted_iota(jnp.int32, (1, LANES), 1) < D_HEAD
    q2 = q_ref[...]
    bias = bias_ref[...]
    tri = tri_ref[...]
    acc_ref[...] = jnp.zeros_like(acc_ref)
    c_ref[...] = jnp.zeros_like(c_ref)

    def tile(kb, diagonal):
        ks = pl.multiple_of(kb * blk, blk)
        k2 = k_ref[pl.ds(ks, blk), :]
        v2 = v_ref[pl.ds(ks, blk), :]
        z = _dot_nt(q2, _head_split(k2, first)) + bias
        lb, lr = _log_sigmoid_pair(z)
        if diagonal:
            qi = lax.broadcasted_iota(jnp.int32, (blk, blk), 0)
            ki = lax.broadcasted_iota(jnp.int32, (blk, blk), 1)
            keep = ki < qi
            lr = jnp.where(jnp.concatenate([keep, keep], axis=1), lr, 0.0)
        hi, lo = _split2(lr)
        probs = []
        for h in range(2):
            sl = slice(h * blk, (h + 1) * blk)
            r = _dot(jnp.concatenate([hi[:, sl], lo[:, sl]], axis=1), tri)
            a = jnp.exp(lb[:, sl] + r[:, :blk] + c_ref[h])
            if diagonal:
                a = jnp.where(keep, a, 0.0)
            probs.append(a.astype(BF16))
            c_ref[h] += r[:, blk:]
        acc_ref[...] += _dot(jnp.concatenate(probs, axis=1), _head_split(v2, first))

    tile(i, True)

    def body(step, carry):
        tile(i - 1 - step, False)
        return carry

    lax.fori_loop(0, i, body, 0)
    o_ref[...] = acc_ref[...].astype(o_ref.dtype)


def _sb_prompt(sq, skb, svb, bias, blk=128):
    b, s, _ = sq.shape
    bias_pairs = jnp.repeat(bias.astype(F32), blk).reshape(PAIRS, 1, 2 * blk)
    kv = pl.BlockSpec((None, s, LANES), lambda bi, p, i: (bi, 0, p))
    qo = pl.BlockSpec((None, blk, LANES), lambda bi, p, i: (bi, i, p))
    return pl.pallas_call(
        functools.partial(_sb_prompt_kernel, blk=blk),
        grid=(b, PAIRS, s // blk),
        in_specs=[pl.BlockSpec((None, 1, 2 * blk), lambda bi, p, i: (p, 0, 0)),
                  _const_spec((2 * blk, 2 * blk)), qo, kv, kv],
        out_specs=qo,
        out_shape=jax.ShapeDtypeStruct((b, s, D_SB), BF16),
        scratch_shapes=[pltpu.VMEM((blk, LANES), F32), pltpu.VMEM((2, blk, blk), F32)],
        compiler_params=pltpu.CompilerParams(dimension_semantics=("parallel", "parallel", "arbitrary"),
                                             vmem_limit_bytes=VMEM_LIMIT),
        name="sb_prompt",
    )(bias_pairs, _suffix_matrix(blk), sq, skb, svb)


def _ret_sample_kernel(q_ref, k_ref, kt_ref, v_ref, s_ref, dec_ref, cross_ref, kdec_ref, sdec_ref, o_ref, so_ref):
    q = q_ref[...]
    v = v_ref[...]
    s = s_ref[...]
    bdot = lambda a, b_, dims: lax.dot_general(a, b_, (dims, ((0,), (0,))), preferred_element_type=F32)
    qk = bdot(q, k_ref[...].astype(BF16), ((2,), (2,)))
    prob = (qk * dec_ref[...]).astype(BF16)
    o = bdot(prob, v, ((2,), (1,)))
    o = o + bdot(q, s.astype(BF16), ((2,), (1,))) * cross_ref[...]
    o_ref[...] = o
    kd = (kt_ref[...] * kdec_ref[...]).astype(BF16)
    so_ref[...] = sdec_ref[...] * s + bdot(kd, v, ((2,), (1,)))


def _ret_sample(rq, rk, rv, state, lg, seq_blk=8, t_pad=16):
    bs = state.shape[0]
    t = rq.shape[0] // bs
    n = bs * H_RET
    nb = seq_blk * H_RET

    def heads(x):
        x = x.reshape(bs, t, H_RET, D_HEAD).transpose(0, 2, 1, 3).reshape(n, t, D_HEAD)
        return jnp.pad(x, ((0, 0), (0, t_pad - t), (0, 0)))

    decay, cross, kdec, sdec = _ret_tables(lg, t)
    pad_t = lambda x, axes: jnp.pad(x, [(0, t_pad - t) if a in axes else (0, 0) for a in range(x.ndim)])
    per_blk = lambda x, shape: jnp.tile(jnp.broadcast_to(x, (H_RET,) + shape), (seq_blk, 1, 1))
    dec_t = per_blk(pad_t(decay, (1, 2)), (t_pad, t_pad))
    cross_t = per_blk(pad_t(cross, (1,))[:, :, None], (t_pad, D_HEAD))
    kdec_t = per_blk(pad_t(kdec, (1,))[:, None, :], (D_HEAD, t_pad))
    sdec_t = per_blk(sdec[:, None, None], (D_HEAD, D_HEAD))
    k_heads = heads(rk)
    qkv = pl.BlockSpec((nb, t_pad, D_HEAD), lambda i: (i, 0, 0))
    ktr = pl.BlockSpec((nb, D_HEAD, t_pad), lambda i: (i, 0, 0))
    st = pl.BlockSpec((nb, D_HEAD, D_HEAD), lambda i: (i, 0, 0))
    o, s_new = pl.pallas_call(
        _ret_sample_kernel,
        grid=(bs // seq_blk,),
        in_specs=[qkv, qkv, ktr, qkv, st, _const_spec(dec_t.shape), _const_spec(cross_t.shape),
                  _const_spec(kdec_t.shape), _const_spec(sdec_t.shape)],
        out_specs=[qkv, st],
        out_shape=[jax.ShapeDtypeStruct((n, t_pad, D_HEAD), F32),
                   jax.ShapeDtypeStruct((n, D_HEAD, D_HEAD), F32)],
        compiler_params=pltpu.CompilerParams(dimension_semantics=("parallel",), vmem_limit_bytes=VMEM_LIMIT),
        name="ret_sample",
    )(heads(rq), k_heads, k_heads.transpose(0, 2, 1), heads(rv), state.reshape(n, D_HEAD, D_HEAD),
      dec_t, cross_t, kdec_t, sdec_t)
    o = o[:, :t].reshape(bs, H_RET, t, D_HEAD).transpose(0, 2, 1, 3).reshape(bs * t, D_RET)
    return o, s_new.reshape(bs, H_RET, D_HEAD, D_HEAD)


def _sb_sample_kernel(pt_ref, qbd_ref, bias_ref, tri_ref, kn_ref, vn_ref, *rest, pages_per_step, page, t):
    kp = rest[:pages_per_step]
    vp = rest[pages_per_step:2 * pages_per_step]
    o_ref, acc_ref, c_ref = rest[2 * pages_per_step:]
    step = pl.program_id(1)
    qbd = qbd_ref[...]
    bias = bias_ref[...]
    tri = tri_ref[...]
    nq = qbd.shape[0]

    def block(k, v, keep):
        z = _dot_nt(qbd, k.astype(BF16)) + bias
        lb, lr = _log_sigmoid_pair(z)
        if keep is not None:
            lr = jnp.where(keep, lr, 0.0)
        hi, lo = _split2(lr)
        r = _dot(jnp.concatenate([hi, lo], axis=1), tri)
        a = jnp.exp(lb + r[:, :page] + c_ref[...])
        if keep is not None:
            a = jnp.where(keep, a, 0.0)
        c_ref[...] += r[:, page:]
        acc_ref[...] += _dot(a.astype(BF16), v.astype(BF16))

    @pl.when(step == 0)
    def _():
        acc_ref[...] = jnp.zeros_like(acc_ref)
        c_ref[...] = jnp.zeros_like(c_ref)
        pad = jnp.zeros((page - kn_ref.shape[0], D_SB), F32)
        qt = lax.broadcasted_iota(jnp.int32, (nq, page), 0) // H_SB
        key = lax.broadcasted_iota(jnp.int32, (nq, page), 1)
        block(jnp.concatenate([kn_ref[...], pad], axis=0), jnp.concatenate([vn_ref[...], pad], axis=0), key < qt)

    for g in range(pages_per_step):
        block(kp[g][...], vp[g][...], None)

    @pl.when(step == pl.num_programs(1) - 1)
    def _():
        acc = acc_ref[...]
        row_head = lax.broadcasted_iota(jnp.int32, acc.shape, 0) % H_SB
        col_head = lax.broadcasted_iota(jnp.int32, acc.shape, 1) // D_HEAD
        own = jnp.where(row_head == col_head, acc, 0.0)
        o_ref[...] = jnp.sum(own.reshape(t, H_SB, D_SB), axis=1).astype(o_ref.dtype)


def _sb_sample(sq, sk, sv, cache_k, cache_v, page_table, bias, pages_per_step=8, t_pad=8):
    bs, n_pages = page_table.shape
    t = sq.shape[0] // bs
    n_phys, page = cache_k.shape[:2]
    nq = t * H_SB
    eye = jnp.eye(H_SB, dtype=sq.dtype)
    qbd = (sq.reshape(bs, t, 1, H_SB, D_HEAD) * eye[None, None, :, :, None]).reshape(bs, nq, D_SB)
    bias_rows = jnp.broadcast_to(jnp.tile(bias.astype(F32), t)[:, None], (nq, page))
    new = lambda x: jnp.pad(x.reshape(bs, t, D_SB), ((0, 0), (0, t_pad - t), (0, 0)))
    n_steps = n_pages // pages_per_step

    def page_spec(g):
        return pl.BlockSpec((None, page, D_SB),
                            lambda b, s, pt: (pt[b, n_pages - 1 - (s * pages_per_step + g)], 0, 0))

    per_seq = lambda shape: pl.BlockSpec((None,) + shape, lambda b, s, pt: (b, 0, 0))
    const = lambda shape: pl.BlockSpec(shape, lambda b, s, pt: (0,) * len(shape))
    ck = cache_k.reshape(n_phys, page, D_SB)
    cv = cache_v.reshape(n_phys, page, D_SB)
    grid_spec = pltpu.PrefetchScalarGridSpec(
        num_scalar_prefetch=1,
        grid=(bs, n_steps),
        in_specs=[per_seq((nq, D_SB)), const((nq, page)), const((2 * page, 2 * page)),
                  per_seq((t_pad, D_SB)), per_seq((t_pad, D_SB))]
                 + [page_spec(g) for g in range(pages_per_step)] * 2,
        out_specs=per_seq((t, D_SB)),
        scratch_shapes=[pltpu.VMEM((nq, D_SB), F32), pltpu.VMEM((nq, page), F32)],
    )
    o = pl.pallas_call(
        functools.partial(_sb_sample_kernel, pages_per_step=pages_per_step, page=page, t=t),
        grid_spec=grid_spec,
        out_shape=jax.ShapeDtypeStruct((bs, t, D_SB), BF16),
        compiler_params=pltpu.CompilerParams(dimension_semantics=("parallel", "arbitrary"),
                                             vmem_limit_bytes=VMEM_LIMIT),
        name="sb_sample",
    )(page_table, qbd, bias_rows, _suffix_matrix(page), new(sk), new(sv),
      *([ck] * pages_per_step), *([cv] * pages_per_step))
    return o.reshape(bs * t, D_SB)


def _merge_ffn_kernel(x_ref, oret_ref, g_ref, osb_ref, gm_ref, rnw_ref, wo_ref, n2w_ref, wg_ref, wu_ref, wd_ref,
                      fnw_ref, y_ref, *, ff_chunks):
    gm = gm_ref[...]

    def head_mean(v):
        hi, mid, lo = _split3(v)
        return _dot(hi, gm) + _dot(mid, gm) + _dot(lo, gm)

    o = oret_ref[...]
    d = o - head_mean(o)
    n = d * lax.rsqrt(head_mean(d * d) + EPS) * rnw_ref[...]
    g = g_ref[...]
    r = g * (1.0 / (1.0 + jnp.exp(-g))) * n
    x = x_ref[...]
    x = x + _dot(r.astype(BF16), wo_ref[:D_RET, :]) + _dot(osb_ref[...], wo_ref[D_RET:, :])
    h = _rms_scale(x, n2w_ref[...]).astype(BF16)
    d_ff = wg_ref.shape[1]
    fc = d_ff // ff_chunks
    down = None
    for c in range(ff_chunks):
        sl = slice(c * fc, (c + 1) * fc)
        gate = _dot(h, wg_ref[:, sl])
        act = gate * (1.0 / (1.0 + jnp.exp(-gate))) * _dot(h, wu_ref[:, sl])
        part = _dot(act.astype(BF16), wd_ref[sl, :])
        down = part if down is None else down + part
    x = x + down
    y_ref[...] = _rms_scale(x, fnw_ref[...])


def _merge_ffn(x, o_ret, g, o_sb, ret_norm_w, w_out, norm2_w, w_gate, w_up, w_down, final_norm_w, tm):
    n, dm = x.shape
    d_ff = w_gate.shape[1]
    head = np.arange(D_RET) // D_HEAD
    gm = jnp.asarray((head[:, None] == head[None, :]) / D_HEAD, dtype=BF16)
    row = lambda width: pl.BlockSpec((tm, width), lambda i: (i, 0))
    vec = lambda w: w.reshape(1, -1).astype(F32)
    return pl.pallas_call(
        functools.partial(_merge_ffn_kernel, ff_chunks=2),
        grid=(n // tm,),
        in_specs=[row(dm), row(D_RET), row(D_RET), row(D_SB), _const_spec((D_RET, D_RET)), _const_spec((1, D_RET)),
                  _const_spec(w_out.shape), _const_spec((1, dm)), _const_spec(w_gate.shape),
                  _const_spec(w_up.shape), _const_spec(w_down.shape), _const_spec((1, dm))],
        out_specs=row(dm),
        out_shape=jax.ShapeDtypeStruct((n, dm), F32),
        compiler_params=pltpu.CompilerParams(dimension_semantics=("parallel",), vmem_limit_bytes=VMEM_LIMIT),
        name="merge_ffn",
    )(x, o_ret, g, o_sb, gm, vec(ret_norm_w), w_out, vec(norm2_w), w_gate, w_up, w_down, vec(final_norm_w))


def _row_tile(n, want):
    return math.gcd(n, want)


def kernel(x_prompt, x_sample, cache_k, cache_v, state_ret, page_table, norm1_w, w_in, sb_bias, ret_norm_w, w_out,
           norm2_w, w_gate, w_up, w_down, final_norm_w):
    depth = w_in.shape[0]
    assert depth == 1, "single-layer step"
    bp, s, dm = x_prompt.shape
    bs, t, _ = x_sample.shape
    n_pages = page_table.shape[1]
    past_len = n_pages * cache_k.shape[2]
    lg = jnp.log1p(-jnp.exp2(-5.0 - jnp.arange(H_RET, dtype=F32)))
    l = 0
    w_in_b, w_out_b = w_in[l].astype(BF16), w_out[l].astype(BF16)
    w_gate_b, w_up_b, w_down_b = w_gate[l].astype(BF16), w_up[l].astype(BF16), w_down[l].astype(BF16)
    tail = (ret_norm_w[l], w_out_b, norm2_w[l], w_gate_b, w_up_b, w_down_b, final_norm_w)

    xp = x_prompt.reshape(bp * s, dm)
    tm = _row_tile(s, 512)
    rq, rk, rv, rg, sq, sk, sv, skb, svb = _in_proj(xp, norm1_w[l], w_in_b, jnp.arange(s), tm)
    seq = lambda a: a.reshape(bp, s, -1)
    o_ret, s_p = _ret_prompt(seq(rq), seq(rk), seq(rv), lg)
    o_sb = _sb_prompt(seq(sq), seq(skb), seq(svb), sb_bias[l])
    y_prompt = _merge_ffn(xp, o_ret.reshape(bp * s, D_RET), rg, o_sb.reshape(bp * s, D_SB), *tail, tm=tm)

    xs = x_sample.reshape(bs * t, dm)
    tms = _row_tile(bs * t, 512)
    pos_s = jnp.tile(past_len + jnp.arange(t), tms // t)
    rq_s, rk_s, rv_s, rg_s, sq_s, sk_s, sv_s, _, _ = _in_proj(xs, norm1_w[l], w_in_b, pos_s, tms)
    o_ret_s, s_s = _ret_sample(rq_s, rk_s, rv_s, state_ret[l].astype(F32), lg)
    o_sb_s = _sb_sample(sq_s, sk_s, sv_s, cache_k[l], cache_v[l], page_table, sb_bias[l])
    y_sample = _merge_ffn(xs, o_ret_s, rg_s, o_sb_s, *tail, tm=tms)

    kv = lambda a, b_, n: a.reshape(1, b_, n, H_SB, D_HEAD)
    return (y_prompt.reshape(bp, s, dm), y_sample.reshape(bs, t, dm),
            kv(sk, bp, s).astype(cache_k.dtype), kv(sv, bp, s).astype(cache_v.dtype),
            s_p[None].astype(state_ret.dtype),
            kv(sk_s, bs, t).astype(cache_k.dtype), kv(sv_s, bs, t).astype(cache_v.dtype),
            s_s[None].astype(state_ret.dtype))
```

```python
import functools
import math

import jax
import jax.numpy as jnp
import numpy as np
from jax import lax
from jax.experimental import pallas as pl
from jax.experimental.pallas import tpu as pltpu

D_HEAD = 64
H_RET = 8
H_SB = 8
D_RET = H_RET * D_HEAD
D_SB = H_SB * D_HEAD
RET_CHUNK = 128
ROPE_BASE = 10000.0
EPS = 1e-6
LANES = 128
PAIRS = D_SB // LANES
VMEM_LIMIT = 56 * 1024 * 1024
LOG2E = 1.4426950408889634

F32 = jnp.float32
BF16 = jnp.bfloat16


def _dot(a, b):
    return jnp.dot(a, b, preferred_element_type=F32)


def _dot_nt(a, b):
    return lax.dot_general(a, b, (((1,), (1,)), ((), ())), preferred_element_type=F32)


def _dot_tn(a, b):
    return lax.dot_general(a, b, (((0,), (0,)), ((), ())), preferred_element_type=F32)


def _split2(x):
    hi = x.astype(BF16)
    lo = (x - hi.astype(F32)).astype(BF16)
    return hi, lo


def _split3(x):
    hi = x.astype(BF16)
    r = x - hi.astype(F32)
    mid = r.astype(BF16)
    lo = (r - mid.astype(F32)).astype(BF16)
    return hi, mid, lo


def _head_split(x2, lane_is_first):
    zero = jnp.zeros_like(x2)
    return jnp.concatenate([jnp.where(lane_is_first, x2, zero), jnp.where(lane_is_first, zero, x2)], axis=0)


def _log2_sigmoid_pair(zl):
    sign_bit = jnp.int32(-2 ** 31)
    neg_abs = pltpu.bitcast(pltpu.bitcast(zl, jnp.int32) | sign_bit, F32)
    lb = jnp.minimum(zl, 0.0) - jnp.log2(1.0 + jnp.exp2(neg_abs))
    return lb, lb - zl


def _rms_scale(x, w):
    ms = jnp.mean(x * x, axis=-1, keepdims=True)
    return x * lax.rsqrt(ms + EPS) * w


def _const_spec(shape):
    nd = len(shape)
    return pl.BlockSpec(shape, lambda *_: (0,) * nd, pipeline_mode=pl.Buffered(1))


def _in_proj_kernel(x_ref, nw_ref, w_ref, cos_ref, sin_ref,
                    rq_ref, rk_ref, rv_ref, rg_ref, sq_ref, sk_ref, sv_ref, skb_ref, svb_ref):
    h = _rms_scale(x_ref[...], nw_ref[...]).astype(BF16)
    cos = cos_ref[...]
    sin = sin_ref[...]
    first_half = (lax.broadcasted_iota(jnp.int32, (1, LANES), 1) % D_HEAD) < (D_HEAD // 2)

    def group(g):
        return _dot(h, w_ref[:, g * D_RET:(g + 1) * D_RET])

    def rope(z):
        outs = []
        for c in range(D_RET // LANES):
            zc = z[:, c * LANES:(c + 1) * LANES]
            swapped = jnp.where(first_half,
                                pltpu.roll(zc, LANES - D_HEAD // 2, axis=1),
                                pltpu.roll(zc, D_HEAD // 2, axis=1))
            outs.append(zc * cos + swapped * sin)
        return jnp.concatenate(outs, axis=1)

    scale = D_HEAD ** -0.5
    rq_ref[...] = rope(group(0)).astype(BF16)
    rk_ref[...] = rope(group(1)) * scale
    rv_ref[...] = group(2).astype(BF16)
    rg_ref[...] = group(3)
    sq_ref[...] = (group(4) * scale).astype(BF16)
    sk = group(5)
    sk_ref[...] = sk
    skb_ref[...] = sk.astype(BF16)
    sv = group(6)
    sv_ref[...] = sv
    svb_ref[...] = sv.astype(BF16)


def _rope_tables(pos):
    half = D_HEAD // 2
    inv = ROPE_BASE ** (-(jnp.arange(half, dtype=F32) / half))
    ang = pos.astype(F32)[:, None] * inv[None, :]
    cos, sin = jnp.cos(ang), jnp.sin(ang)
    reps = LANES // D_HEAD
    return (jnp.tile(jnp.concatenate([cos, cos], axis=1), (1, reps)),
            jnp.tile(jnp.concatenate([-sin, sin], axis=1), (1, reps)))


def _in_proj(x, norm_w, w_bf16, pos, tm):
    n, dm = x.shape
    cos, sin = _rope_tables(pos)
    pos_tiles = pos.shape[0] // tm
    row = lambda i: (i, 0)
    tab = lambda i: (i % pos_tiles, 0)
    f32_out = jax.ShapeDtypeStruct((n, D_RET), F32)
    bf_out = jax.ShapeDtypeStruct((n, D_RET), BF16)
    out_spec = pl.BlockSpec((tm, D_RET), row)
    return pl.pallas_call(
        _in_proj_kernel,
        grid=(n // tm,),
        in_specs=[pl.BlockSpec((tm, dm), row), _const_spec((1, dm)), _const_spec(w_bf16.shape),
                  pl.BlockSpec((tm, LANES), tab), pl.BlockSpec((tm, LANES), tab)],
        out_specs=[out_spec] * 9,
        out_shape=[bf_out, f32_out, bf_out, f32_out, bf_out, f32_out, f32_out, bf_out, bf_out],
        compiler_params=pltpu.CompilerParams(dimension_semantics=("parallel",), vmem_limit_bytes=VMEM_LIMIT),
        name="in_proj",
    )(x, norm_w.reshape(1, dm), w_bf16, cos, sin)


def _ret_tables(lg, c):
    idx = jnp.arange(c, dtype=F32)
    diff = idx[:, None] - idx[None, :]
    decay = jnp.where(diff >= 0, jnp.exp(lg[:, None, None] * jnp.maximum(diff, 0.0)), 0.0)
    cross = jnp.exp(lg[:, None] * (idx[None, :] + 1.0))
    kdec = jnp.exp(lg[:, None] * (c - 1.0 - idx[None, :]))
    sdec = jnp.exp(lg * c)
    return decay, cross, kdec, sdec


def _ret_prompt_kernel(q_ref, k_ref, v_ref, dec_ref, cross_ref, kdec_ref, sdec_ref, o_ref, s_ref, s_scr):
    c = pl.program_id(1)

    @pl.when(c == 0)
    def _():
        s_scr[...] = jnp.zeros_like(s_scr)

    lane = lax.broadcasted_iota(jnp.int32, (1, LANES), 1)
    first = lane < D_HEAD
    row_first = lax.broadcasted_iota(jnp.int32, (LANES, 1), 0) < D_HEAD
    same_head = row_first == first

    for p in range(PAIRS):
        sl = slice(p * LANES, (p + 1) * LANES)
        q2 = q_ref[:, sl]
        k2f = k_ref[:, sl]
        v2 = v_ref[:, sl]
        qk = _dot_nt(q2, _head_split(k2f.astype(BF16), first))
        prob = (qk * dec_ref[p]).astype(BF16)
        o = _dot(prob, _head_split(v2, first))
        s = s_scr[p]
        o = o + _dot(q2, s.astype(BF16)) * cross_ref[:, sl]
        o_ref[:, sl] = o
        kd = (k2f * kdec_ref[:, sl]).astype(BF16)
        upd = _dot_tn(kd, v2)
        s_scr[p] = sdec_ref[p] * s + jnp.where(same_head, upd, 0.0)

    @pl.when(c == pl.num_programs(1) - 1)
    def _():
        s_ref[...] = s_scr[...]


def _ret_prompt(rq, rk, rv, lg):
    b, s, _ = rq.shape
    c = RET_CHUNK
    decay, cross, kdec, sdec = _ret_tables(lg, c)
    dec_pairs = decay.reshape(PAIRS, 2, c, c).transpose(0, 2, 1, 3).reshape(PAIRS, c, 2 * c)
    lanes = lambda t: jnp.repeat(t.T, D_HEAD, axis=1)
    sdec_pairs = jnp.broadcast_to(jnp.repeat(sdec, D_HEAD).reshape(PAIRS, LANES, 1), (PAIRS, LANES, LANES))
    blk = pl.BlockSpec((None, c, D_RET), lambda bi, ci: (bi, ci, 0))
    o, st = pl.pallas_call(
        _ret_prompt_kernel,
        grid=(b, s // c),
        in_specs=[blk, blk, blk, _const_spec((PAIRS, c, 2 * c)), _const_spec((c, D_RET)),
                  _const_spec((c, D_RET)), _const_spec((PAIRS, LANES, LANES))],
        out_specs=[blk, pl.BlockSpec((None, PAIRS, LANES, LANES), lambda bi, ci: (bi, 0, 0, 0))],
        out_shape=[jax.ShapeDtypeStruct((b, s, D_RET), F32),
                   jax.ShapeDtypeStruct((b, PAIRS, LANES, LANES), F32)],
        scratch_shapes=[pltpu.VMEM((PAIRS, LANES, LANES), F32)],
        compiler_params=pltpu.CompilerParams(dimension_semantics=("parallel", "arbitrary"),
                                             vmem_limit_bytes=VMEM_LIMIT),
        name="ret_prompt",
    )(rq, rk, rv, dec_pairs, lanes(cross), lanes(kdec), sdec_pairs)
    st = st.reshape(b, PAIRS, 2, D_HEAD, 2, D_HEAD)
    st = jnp.stack([st[:, :, 0, :, 0, :], st[:, :, 1, :, 1, :]], axis=2).reshape(b, H_RET, D_HEAD, D_HEAD)
    return o, st


def _suffix_matrix(n):
    j = np.arange(2 * n)[:, None] % n
    s = np.arange(2 * n)[None, :]
    return jnp.asarray(np.where(s < n, j > s, True), dtype=BF16)


def _sb_prompt_kernel(bias_ref, tri_ref, q_ref, k_ref, v_ref, o_ref, acc_ref, c_ref, qk_ref, p_ref, qs_ref, *,
                      tq, tk):
    i = pl.program_id(2)
    first = lax.broadcasted_iota(jnp.int32, (1, LANES), 1) < D_HEAD
    bias = bias_ref[...]
    tri = tri_ref[...]
    acc_ref[...] = jnp.zeros_like(acc_ref)
    c_ref[...] = jnp.zeros_like(c_ref)
    diag_blocks = tq // tk
    qs_ref[...] = jnp.concatenate(_split2(q_ref[...].astype(F32) * LOG2E), axis=1)

    def scores(kb, row0):
        kz = _head_split(k_ref[pl.ds(pl.multiple_of(kb * tk, tk), tk), :], first)
        return _dot_nt(qs_ref[row0:, :], jnp.concatenate([kz, kz], axis=1))

    def weights(qk, row0, diagonal):
        rows = slice(row0, tq)
        lb, lr = _log2_sigmoid_pair(qk + bias)
        if diagonal:
            qi = lax.broadcasted_iota(jnp.int32, (tq - row0, tk), 0)
            ki = lax.broadcasted_iota(jnp.int32, (tq - row0, tk), 1)
            keep = ki < qi
            lr = jnp.where(jnp.concatenate([keep, keep], axis=1), lr, 0.0)
        hi, lo = _split2(lr)
        probs = []
        for h in range(2):
            sl = slice(h * tk, (h + 1) * tk)
            r = _dot(jnp.concatenate([hi[:, sl], lo[:, sl]], axis=1), tri)
            a = jnp.exp2(lb[:, sl] + r[:, :tk] + c_ref[h, rows, :])
            if diagonal:
                a = jnp.where(keep, a, 0.0)
            probs.append(a.astype(BF16))
            c_ref[h, rows, :] += r[:, tk:]
        return jnp.concatenate(probs, axis=1)

    def accumulate(kb, probs, row0):
        v2 = v_ref[pl.ds(pl.multiple_of(kb * tk, tk), tk), :]
        acc_ref[row0:, :] += _dot(probs, _head_split(v2, first))

    n_bulk = i * diag_blocks
    for d in reversed(range(diag_blocks)):
        accumulate(n_bulk + d, weights(scores(n_bulk + d, d * tk), d * tk, True), d * tk)

    qk_ref[0] = scores(jnp.maximum(n_bulk - 1, 0), 0)
    p_ref[1] = jnp.zeros((tq, 2 * tk), BF16)

    def body(pair, carry):
        kb = n_bulk - 1 - 2 * pair
        for cur in range(2):
            qk_ref[1 - cur] = scores(jnp.maximum(kb - cur - 1, 0), 0)
            accumulate(kb - cur + 1, p_ref[1 - cur], 0)
            p_ref[cur] = weights(qk_ref[cur], 0, False)
        return carry

    lax.fori_loop(0, n_bulk // 2, body, 0)
    accumulate(0, p_ref[1], 0)
    o_ref[...] = acc_ref[...].astype(o_ref.dtype)


def _sb_prompt(sq, skb, svb, bias, tq=512, tk=128):
    b, s, _ = sq.shape
    tq = math.gcd(s, tq)
    bias_pairs = jnp.repeat(bias.astype(F32) * LOG2E, tk).reshape(PAIRS, 1, 2 * tk)
    kv = pl.BlockSpec((None, s, LANES), lambda bi, p, i: (bi, 0, p))
    qo = pl.BlockSpec((None, tq, LANES), lambda bi, p, i: (bi, i, p))
    return pl.pallas_call(
        functools.partial(_sb_prompt_kernel, tq=tq, tk=tk),
        grid=(b, PAIRS, s // tq),
        in_specs=[pl.BlockSpec((None, 1, 2 * tk), lambda bi, p, i: (p, 0, 0)),
                  _const_spec((2 * tk, 2 * tk)), qo, kv, kv],
        out_specs=qo,
        out_shape=jax.ShapeDtypeStruct((b, s, D_SB), BF16),
        scratch_shapes=[pltpu.VMEM((tq, LANES), F32), pltpu.VMEM((2, tq, tk), F32),
                        pltpu.VMEM((2, tq, 2 * tk), F32), pltpu.VMEM((2, tq, 2 * tk), BF16),
                        pltpu.VMEM((tq, 2 * LANES), BF16)],
        compiler_params=pltpu.CompilerParams(dimension_semantics=("parallel", "parallel", "arbitrary"),
                                             vmem_limit_bytes=VMEM_LIMIT),
        name="sb_prompt",
    )(bias_pairs, _suffix_matrix(tk), sq, skb, svb)


def _ret_sample_kernel(q_ref, k_ref, kt_ref, v_ref, s_ref, dec_ref, cross_ref, kdec_ref, sdec_ref, o_ref, so_ref):
    q = q_ref[...]
    v = v_ref[...]
    s = s_ref[...]
    bdot = lambda a, b_, dims: lax.dot_general(a, b_, (dims, ((0,), (0,))), preferred_element_type=F32)
    qk = bdot(q, k_ref[...].astype(BF16), ((2,), (2,)))
    prob = (qk * dec_ref[...]).astype(BF16)
    o = bdot(prob, v, ((2,), (1,)))
    o = o + bdot(q, s.astype(BF16), ((2,), (1,))) * cross_ref[...]
    o_ref[...] = o
    kd = (kt_ref[...] * kdec_ref[...]).astype(BF16)
    so_ref[...] = sdec_ref[...] * s + bdot(kd, v, ((2,), (1,)))


def _ret_sample(rq, rk, rv, state, lg, seq_blk=8, t_pad=16):
    bs = state.shape[0]
    t = rq.shape[0] // bs
    n = bs * H_RET
    nb = seq_blk * H_RET

    def heads(x):
        x = x.reshape(bs, t, H_RET, D_HEAD).transpose(0, 2, 1, 3).reshape(n, t, D_HEAD)
        return jnp.pad(x, ((0, 0), (0, t_pad - t), (0, 0)))

    decay, cross, kdec, sdec = _ret_tables(lg, t)
    pad_t = lambda x, axes: jnp.pad(x, [(0, t_pad - t) if a in axes else (0, 0) for a in range(x.ndim)])
    per_blk = lambda x, shape: jnp.tile(jnp.broadcast_to(x, (H_RET,) + shape), (seq_blk, 1, 1))
    dec_t = per_blk(pad_t(decay, (1, 2)), (t_pad, t_pad))
    cross_t = per_blk(pad_t(cross, (1,))[:, :, None], (t_pad, D_HEAD))
    kdec_t = per_blk(pad_t(kdec, (1,))[:, None, :], (D_HEAD, t_pad))
    sdec_t = per_blk(sdec[:, None, None], (D_HEAD, D_HEAD))
    k_heads = heads(rk)
    qkv = pl.BlockSpec((nb, t_pad, D_HEAD), lambda i: (i, 0, 0))
    ktr = pl.BlockSpec((nb, D_HEAD, t_pad), lambda i: (i, 0, 0))
    st = pl.BlockSpec((nb, D_HEAD, D_HEAD), lambda i: (i, 0, 0))
    o, s_new = pl.pallas_call(
        _ret_sample_kernel,
        grid=(bs // seq_blk,),
        in_specs=[qkv, qkv, ktr, qkv, st, _const_spec(dec_t.shape), _const_spec(cross_t.shape),
                  _const_spec(kdec_t.shape), _const_spec(sdec_t.shape)],
        out_specs=[qkv, st],
        out_shape=[jax.ShapeDtypeStruct((n, t_pad, D_HEAD), F32),
                   jax.ShapeDtypeStruct((n, D_HEAD, D_HEAD), F32)],
        compiler_params=pltpu.CompilerParams(dimension_semantics=("parallel",), vmem_limit_bytes=VMEM_LIMIT),
        name="ret_sample",
    )(heads(rq), k_heads, k_heads.transpose(0, 2, 1), heads(rv), state.reshape(n, D_HEAD, D_HEAD),
      dec_t, cross_t, kdec_t, sdec_t)
    o = o[:, :t].reshape(bs, H_RET, t, D_HEAD).transpose(0, 2, 1, 3).reshape(bs * t, D_RET)
    return o, s_new.reshape(bs, H_RET, D_HEAD, D_HEAD)


def _sb_sample_kernel(pt_ref, qbd_ref, bias_ref, tri_ref, kn_ref, vn_ref, *rest, pages_per_step, page, t):
    kp = rest[:pages_per_step]
    vp = rest[pages_per_step:2 * pages_per_step]
    o_ref, acc_ref, c_ref = rest[2 * pages_per_step:]
    step = pl.program_id(1)
    qbd = qbd_ref[...]
    bias = bias_ref[...]
    tri = tri_ref[...]
    nq = qbd.shape[0]

    def block(k, v, keep, transposed):
        k, v = k.astype(BF16), v.astype(BF16)
        qk = _dot(qbd, k) if transposed else _dot_nt(qbd, k)
        zl = qk * LOG2E + bias
        lb, lr = _log2_sigmoid_pair(zl)
        if keep is not None:
            lr = jnp.where(keep, lr, 0.0)
        hi, lo = _split2(lr)
        r = _dot(jnp.concatenate([hi, lo], axis=1), tri)
        a = jnp.exp2(lb + r[:, :page] + c_ref[...])
        if keep is not None:
            a = jnp.where(keep, a, 0.0)
        c_ref[...] += r[:, page:]
        a = a.astype(BF16)
        acc_ref[...] += _dot_nt(a, v) if transposed else _dot(a, v)

    @pl.when(step == 0)
    def _():
        acc_ref[...] = jnp.zeros_like(acc_ref)
        c_ref[...] = jnp.zeros_like(c_ref)
        pad = jnp.zeros((page - kn_ref.shape[0], D_SB), F32)
        qt = lax.broadcasted_iota(jnp.int32, (nq, page), 0) // H_SB
        key = lax.broadcasted_iota(jnp.int32, (nq, page), 1)
        block(jnp.concatenate([kn_ref[...], pad], axis=0), jnp.concatenate([vn_ref[...], pad], axis=0),
              key < qt, False)

    for g in range(pages_per_step):
        block(kp[g][...], vp[g][...], None, True)

    @pl.when(step == pl.num_programs(1) - 1)
    def _():
        acc = acc_ref[...]
        row_head = lax.broadcasted_iota(jnp.int32, acc.shape, 0) % H_SB
        col_head = lax.broadcasted_iota(jnp.int32, acc.shape, 1) // D_HEAD
        own = jnp.where(row_head == col_head, acc, 0.0)
        o_ref[...] = jnp.sum(own.reshape(t, H_SB, D_SB), axis=1).astype(o_ref.dtype)


def _sb_sample(sq, sk, sv, cache_k, cache_v, page_table, bias, pages_per_step=8, t_pad=8):
    bs, n_pages = page_table.shape
    t = sq.shape[0] // bs
    n_phys, page = cache_k.shape[:2]
    nq = t * H_SB
    eye = jnp.eye(H_SB, dtype=sq.dtype)
    qbd = (sq.reshape(bs, t, 1, H_SB, D_HEAD) * eye[None, None, :, :, None]).reshape(bs, nq, D_SB)
    bias_rows = jnp.broadcast_to(jnp.tile(bias.astype(F32) * LOG2E, t)[:, None], (nq, page))
    new = lambda x: jnp.pad(x.reshape(bs, t, D_SB), ((0, 0), (0, t_pad - t), (0, 0)))
    n_steps = n_pages // pages_per_step

    def page_spec(g):
        return pl.BlockSpec((None, D_SB, page),
                            lambda b, s, pt: (pt[b, n_pages - 1 - (s * pages_per_step + g)], 0, 0))

    per_seq = lambda shape: pl.BlockSpec((None,) + shape, lambda b, s, pt: (b, 0, 0))
    const = lambda shape: pl.BlockSpec(shape, lambda b, s, pt: (0,) * len(shape))
    ck = cache_k.transpose(0, 2, 3, 1).reshape(n_phys, D_SB, page)
    cv = cache_v.transpose(0, 2, 3, 1).reshape(n_phys, D_SB, page)
    grid_spec = pltpu.PrefetchScalarGridSpec(
        num_scalar_prefetch=1,
        grid=(bs, n_steps),
        in_specs=[per_seq((nq, D_SB)), const((nq, page)), const((2 * page, 2 * page)),
                  per_seq((t_pad, D_SB)), per_seq((t_pad, D_SB))]
                 + [page_spec(g) for g in range(pages_per_step)] * 2,
        out_specs=per_seq((t, D_SB)),
        scratch_shapes=[pltpu.VMEM((nq, D_SB), F32), pltpu.VMEM((nq, page), F32)],
    )
    o = pl.pallas_call(
        functools.partial(_sb_sample_kernel, pages_per_step=pages_per_step, page=page, t=t),
        grid_spec=grid_spec,
        out_shape=jax.ShapeDtypeStruct((bs, t, D_SB), BF16),
        compiler_params=pltpu.CompilerParams(dimension_semantics=("parallel", "arbitrary"),
                                             vmem_limit_bytes=VMEM_LIMIT),
        name="sb_sample",
    )(page_table, qbd, bias_rows, _suffix_matrix(page), new(sk), new(sv),
      *([ck] * pages_per_step), *([cv] * pages_per_step))
    return o.reshape(bs * t, D_SB)


def _merge_ffn_kernel(x_ref, oret_ref, g_ref, osb_ref, gm_ref, rnw_ref, wo_ref, n2w_ref, wg_ref, wu_ref, wd_ref,
                      fnw_ref, y_ref, *, ff_chunks):
    gm = gm_ref[...]

    def head_mean(v):
        hi, mid, lo = _split3(v)
        return _dot(hi, gm) + _dot(mid, gm) + _dot(lo, gm)

    o = oret_ref[...]
    d = o - head_mean(o)
    n = d * lax.rsqrt(head_mean(d * d) + EPS) * rnw_ref[...]
    g = g_ref[...]
    r = g * (1.0 / (1.0 + jnp.exp(-g))) * n
    x = x_ref[...]
    x = x + _dot(r.astype(BF16), wo_ref[:D_RET, :]) + _dot(osb_ref[...], wo_ref[D_RET:, :])
    h = _rms_scale(x, n2w_ref[...]).astype(BF16)
    d_ff = wg_ref.shape[1]
    fc = d_ff // ff_chunks
    down = None
    for c in range(ff_chunks):
        sl = slice(c * fc, (c + 1) * fc)
        gate = _dot(h, wg_ref[:, sl])
        act = gate * (1.0 / (1.0 + jnp.exp(-gate))) * _dot(h, wu_ref[:, sl])
        part = _dot(act.astype(BF16), wd_ref[sl, :])
        down = part if down is None else down + part
    x = x + down
    y_ref[...] = _rms_scale(x, fnw_ref[...])


def _merge_ffn(x, o_ret, g, o_sb, ret_norm_w, w_out, norm2_w, w_gate, w_up, w_down, final_norm_w, tm):
    n, dm = x.shape
    d_ff = w_gate.shape[1]
    head = np.arange(D_RET) // D_HEAD
    gm = jnp.asarray((head[:, None] == head[None, :]) / D_HEAD, dtype=BF16)
    row = lambda width: pl.BlockSpec((tm, width), lambda i: (i, 0))
    vec = lambda w: w.reshape(1, -1).astype(F32)
    return pl.pallas_call(
        functools.partial(_merge_ffn_kernel, ff_chunks=2),
        grid=(n // tm,),
        in_specs=[row(dm), row(D_RET), row(D_RET), row(D_SB), _const_spec((D_RET, D_RET)), _const_spec((1, D_RET)),
                  _const_spec(w_out.shape), _const_spec((1, dm)), _const_spec(w_gate.shape),
                  _const_spec(w_up.shape), _const_spec(w_down.shape), _const_spec((1, dm))],
        out_specs=row(dm),
        out_shape=jax.ShapeDtypeStruct((n, dm), F32),
        compiler_params=pltpu.CompilerParams(dimension_semantics=("parallel",), vmem_limit_bytes=VMEM_LIMIT),
        name="merge_ffn",
    )(x, o_ret, g, o_sb, gm, vec(ret_norm_w), w_out, vec(norm2_w), w_gate, w_up, w_down, vec(final_norm_w))


def _row_tile(n, want):
    return math.gcd(n, want)


def kernel(x_prompt, x_sample, cache_k, cache_v, state_ret, page_table, norm1_w, w_in, sb_bias, ret_norm_w, w_out,
           norm2_w, w_gate, w_up, w_down, final_norm_w):
    depth = w_in.shape[0]
    assert depth == 1, "single-layer step"
    bp, s, dm = x_prompt.shape
    bs, t, _ = x_sample.shape
    n_pages = page_table.shape[1]
    past_len = n_pages * cache_k.shape[2]
    lg = jnp.log1p(-jnp.exp2(-5.0 - jnp.arange(H_RET, dtype=F32)))
    l = 0
    w_in_b, w_out_b = w_in[l].astype(BF16), w_out[l].astype(BF16)
    w_gate_b, w_up_b, w_down_b = w_gate[l].astype(BF16), w_up[l].astype(BF16), w_down[l].astype(BF16)
    tail = (ret_norm_w[l], w_out_b, norm2_w[l], w_gate_b, w_up_b, w_down_b, final_norm_w)

    xp = x_prompt.reshape(bp * s, dm)
    tm = _row_tile(s, 512)
    rq, rk, rv, rg, sq, sk, sv, skb, svb = _in_proj(xp, norm1_w[l], w_in_b, jnp.arange(s), tm)
    seq = lambda a: a.reshape(bp, s, -1)
    o_ret, s_p = _ret_prompt(seq(rq), seq(rk), seq(rv), lg)
    o_sb = _sb_prompt(seq(sq), seq(skb), seq(svb), sb_bias[l])
    y_prompt = _merge_ffn(xp, o_ret.reshape(bp * s, D_RET), rg, o_sb.reshape(bp * s, D_SB), *tail, tm=tm)

    xs = x_sample.reshape(bs * t, dm)
    tms = _row_tile(bs * t, 512)
    pos_s = jnp.tile(past_len + jnp.arange(t), tms // t)
    rq_s, rk_s, rv_s, rg_s, sq_s, sk_s, sv_s, _, _ = _in_proj(xs, norm1_w[l], w_in_b, pos_s, tms)
    o_ret_s, s_s = _ret_sample(rq_s, rk_s, rv_s, state_ret[l].astype(F32), lg)
    o_sb_s = _sb_sample(sq_s, sk_s, sv_s, cache_k[l], cache_v[l], page_table, sb_bias[l])
    y_sample = _merge_ffn(xs, o_ret_s, rg_s, o_sb_s, *tail, tm=tms)

    kv = lambda a, b_, n: a.reshape(1, b_, n, H_SB, D_HEAD)
    return (y_prompt.reshape(bp, s, dm), y_sample.reshape(bs, t, dm),
            kv(sk, bp, s).astype(cache_k.dtype), kv(sv, bp, s).astype(cache_v.dtype),
            s_p[None].astype(state_ret.dtype),
            kv(sk_s, bs, t).astype(cache_k.dtype), kv(sv_s, bs, t).astype(cache_v.dtype),
            s_s[None].astype(state_ret.dtype))
```

```python
import functools
import math

import jax
import jax.numpy as jnp
import numpy as np
from jax import lax
from jax.experimental import pallas as pl
from jax.experimental.pallas import tpu as pltpu

D_HEAD = 64
H_RET = 8
H_SB = 8
D_RET = H_RET * D_HEAD
D_SB = H_SB * D_HEAD
RET_CHUNK = 128
ROPE_BASE = 10000.0
EPS = 1e-6
LANES = 128
PAIRS = D_SB // LANES
VMEM_LIMIT = 56 * 1024 * 1024
LOG2E = 1.4426950408889634

F32 = jnp.float32
BF16 = jnp.bfloat16


def _dot(a, b):
    return jnp.dot(a, b, preferred_element_type=F32)


def _dot_nt(a, b):
    return lax.dot_general(a, b, (((1,), (1,)), ((), ())), preferred_element_type=F32)


def _dot_tn(a, b):
    return lax.dot_general(a, b, (((0,), (0,)), ((), ())), preferred_element_type=F32)


def _split2(x):
    hi = x.astype(BF16)
    lo = (x - hi.astype(F32)).astype(BF16)
    return hi, lo


def _split3(x):
    hi = x.astype(BF16)
    r = x - hi.astype(F32)
    mid = r.astype(BF16)
    lo = (r - mid.astype(F32)).astype(BF16)
    return hi, mid, lo


def _head_split(x2, lane_is_first):
    zero = jnp.zeros_like(x2)
    return jnp.concatenate([jnp.where(lane_is_first, x2, zero), jnp.where(lane_is_first, zero, x2)], axis=0)


def _neg_log2_survival(zl):
    sign_bit = jnp.int32(-2 ** 31)
    neg_abs = pltpu.bitcast(pltpu.bitcast(zl, jnp.int32) | sign_bit, F32)
    return jnp.maximum(zl, 0.0) + jnp.log2(1.0 + jnp.exp2(neg_abs))


def _rms_scale(x, w):
    ms = jnp.mean(x * x, axis=-1, keepdims=True)
    return x * lax.rsqrt(ms + EPS) * w


def _const_spec(shape):
    nd = len(shape)
    return pl.BlockSpec(shape, lambda *_: (0,) * nd, pipeline_mode=pl.Buffered(1))


def _in_proj_kernel(x_ref, nw_ref, w_ref, cos_ref, sin_ref,
                    rq_ref, rk_ref, rv_ref, rg_ref, sq_ref, sk_ref, sv_ref, skb_ref, svb_ref, *, kv_position_minor):
    h = _rms_scale(x_ref[...], nw_ref[...]).astype(BF16)
    cos = cos_ref[...]
    sin = sin_ref[...]
    first_half = (lax.broadcasted_iota(jnp.int32, (1, LANES), 1) % D_HEAD) < (D_HEAD // 2)

    def group(g):
        return _dot(h, w_ref[:, g * D_RET:(g + 1) * D_RET])

    def rope(z):
        outs = []
        for c in range(D_RET // LANES):
            zc = z[:, c * LANES:(c + 1) * LANES]
            swapped = jnp.where(first_half,
                                pltpu.roll(zc, LANES - D_HEAD // 2, axis=1),
                                pltpu.roll(zc, D_HEAD // 2, axis=1))
            outs.append(zc * cos + swapped * sin)
        return jnp.concatenate(outs, axis=1)

    scale = D_HEAD ** -0.5
    rq_ref[...] = rope(group(0)).astype(BF16)
    rk_ref[...] = rope(group(1)) * scale
    rv_ref[...] = group(2).astype(BF16)
    rg_ref[...] = group(3)
    sq_ref[...] = (group(4) * scale).astype(BF16)
    sk = group(5)
    sk_ref[...] = sk.T if kv_position_minor else sk
    skb_ref[...] = sk.astype(BF16)
    sv = group(6)
    sv_ref[...] = sv.T if kv_position_minor else sv
    svb_ref[...] = sv.astype(BF16)


def _rope_tables(pos):
    half = D_HEAD // 2
    inv = ROPE_BASE ** (-(jnp.arange(half, dtype=F32) / half))
    ang = pos.astype(F32)[:, None] * inv[None, :]
    cos, sin = jnp.cos(ang), jnp.sin(ang)
    reps = LANES // D_HEAD
    return (jnp.tile(jnp.concatenate([cos, cos], axis=1), (1, reps)),
            jnp.tile(jnp.concatenate([-sin, sin], axis=1), (1, reps)))


def _in_proj(x, norm_w, w_bf16, pos, tm, kv_position_minor=False):
    n, dm = x.shape
    cos, sin = _rope_tables(pos)
    pos_tiles = pos.shape[0] // tm
    row = lambda i: (i, 0)
    tab = lambda i: (i % pos_tiles, 0)
    f32_out = jax.ShapeDtypeStruct((n, D_RET), F32)
    bf_out = jax.ShapeDtypeStruct((n, D_RET), BF16)
    out_spec = pl.BlockSpec((tm, D_RET), row)
    kv_out, kv_spec = f32_out, out_spec
    if kv_position_minor:
        kv_out = jax.ShapeDtypeStruct((n // pos.shape[0], D_SB, pos.shape[0]), F32)
        kv_spec = pl.BlockSpec((None, D_SB, tm), lambda i: (i // pos_tiles, 0, i % pos_tiles))
    return pl.pallas_call(
        functools.partial(_in_proj_kernel, kv_position_minor=kv_position_minor),
        grid=(n // tm,),
        in_specs=[pl.BlockSpec((tm, dm), row), _const_spec((1, dm)), _const_spec(w_bf16.shape),
                  pl.BlockSpec((tm, LANES), tab), pl.BlockSpec((tm, LANES), tab)],
        out_specs=[out_spec] * 5 + [kv_spec] * 2 + [out_spec] * 2,
        out_shape=[bf_out, f32_out, bf_out, f32_out, bf_out, kv_out, kv_out, bf_out, bf_out],
        compiler_params=pltpu.CompilerParams(dimension_semantics=("parallel",), vmem_limit_bytes=VMEM_LIMIT),
        name="in_proj",
    )(x, norm_w.reshape(1, dm), w_bf16, cos, sin)


def _ret_tables(lg, c):
    idx = jnp.arange(c, dtype=F32)
    diff = idx[:, None] - idx[None, :]
    decay = jnp.where(diff >= 0, jnp.exp(lg[:, None, None] * jnp.maximum(diff, 0.0)), 0.0)
    cross = jnp.exp(lg[:, None] * (idx[None, :] + 1.0))
    kdec = jnp.exp(lg[:, None] * (c - 1.0 - idx[None, :]))
    sdec = jnp.exp(lg * c)
    return decay, cross, kdec, sdec


def _ret_prompt_kernel(q_ref, k_ref, v_ref, dec_ref, cross_ref, kdec_ref, sdec_ref, o_ref, s_ref, s_scr):
    c = pl.program_id(1)

    @pl.when(c == 0)
    def _():
        s_scr[...] = jnp.zeros_like(s_scr)

    lane = lax.broadcasted_iota(jnp.int32, (1, LANES), 1)
    first = lane < D_HEAD
    row_first = lax.broadcasted_iota(jnp.int32, (LANES, 1), 0) < D_HEAD
    same_head = row_first == first

    for p in range(PAIRS):
        sl = slice(p * LANES, (p + 1) * LANES)
        q2 = q_ref[:, sl]
        k2f = k_ref[:, sl]
        v2 = v_ref[:, sl]
        qk = _dot_nt(q2, _head_split(k2f.astype(BF16), first))
        prob = (qk * dec_ref[p]).astype(BF16)
        o = _dot(prob, _head_split(v2, first))
        s = s_scr[p]
        o = o + _dot(q2, s.astype(BF16)) * cross_ref[:, sl]
        o_ref[:, sl] = o
        kd = (k2f * kdec_ref[:, sl]).astype(BF16)
        upd = _dot_tn(kd, v2)
        s_scr[p] = sdec_ref[p] * s + jnp.where(same_head, upd, 0.0)

    @pl.when(c == pl.num_programs(1) - 1)
    def _():
        s_ref[...] = s_scr[...]


def _ret_prompt(rq, rk, rv, lg):
    b, s, _ = rq.shape
    c = RET_CHUNK
    decay, cross, kdec, sdec = _ret_tables(lg, c)
    dec_pairs = decay.reshape(PAIRS, 2, c, c).transpose(0, 2, 1, 3).reshape(PAIRS, c, 2 * c)
    lanes = lambda t: jnp.repeat(t.T, D_HEAD, axis=1)
    sdec_pairs = jnp.broadcast_to(jnp.repeat(sdec, D_HEAD).reshape(PAIRS, LANES, 1), (PAIRS, LANES, LANES))
    blk = pl.BlockSpec((None, c, D_RET), lambda bi, ci: (bi, ci, 0))
    o, st = pl.pallas_call(
        _ret_prompt_kernel,
        grid=(b, s // c),
        in_specs=[blk, blk, blk, _const_spec((PAIRS, c, 2 * c)), _const_spec((c, D_RET)),
                  _const_spec((c, D_RET)), _const_spec((PAIRS, LANES, LANES))],
        out_specs=[blk, pl.BlockSpec((None, PAIRS, LANES, LANES), lambda bi, ci: (bi, 0, 0, 0))],
        out_shape=[jax.ShapeDtypeStruct((b, s, D_RET), F32),
                   jax.ShapeDtypeStruct((b, PAIRS, LANES, LANES), F32)],
        scratch_shapes=[pltpu.VMEM((PAIRS, LANES, LANES), F32)],
        compiler_params=pltpu.CompilerParams(dimension_semantics=("parallel", "arbitrary"),
                                             vmem_limit_bytes=VMEM_LIMIT),
        name="ret_prompt",
    )(rq, rk, rv, dec_pairs, lanes(cross), lanes(kdec), sdec_pairs)
    st = st.reshape(b, PAIRS, 2, D_HEAD, 2, D_HEAD)
    st = jnp.stack([st[:, :, 0, :, 0, :], st[:, :, 1, :, 1, :]], axis=2).reshape(b, H_RET, D_HEAD, D_HEAD)
    return o, st


def _suffix_matrix(n):
    j = np.arange(2 * n)[:, None] % n
    s = np.arange(2 * n)[None, :]
    return jnp.asarray(-np.where(s < n, j >= s, True).astype(np.float32), dtype=BF16)


def _sb_prompt_kernel(bias_ref, tri_ref, q_ref, k_ref, v_ref, o_ref, acc_ref, c_ref, qk_ref, p_ref, qs_ref, *,
                      tq, tk):
    i = pl.program_id(2)
    first = lax.broadcasted_iota(jnp.int32, (1, LANES), 1) < D_HEAD
    bias = bias_ref[...]
    tri = tri_ref[...]
    acc_ref[...] = jnp.zeros_like(acc_ref)
    c_ref[...] = jnp.zeros_like(c_ref)
    diag_blocks = tq // tk
    qs_ref[...] = jnp.concatenate(_split2(q_ref[...].astype(F32) * LOG2E), axis=1)

    def scores(kb, row0):
        kz = _head_split(k_ref[pl.ds(pl.multiple_of(kb * tk, tk), tk), :], first)
        return _dot_nt(qs_ref[row0:, :], jnp.concatenate([kz, kz], axis=1))

    def weights(qk, row0, diagonal):
        rows = slice(row0, tq)
        zl = qk + bias
        u = _neg_log2_survival(zl)
        if diagonal:
            qi = lax.broadcasted_iota(jnp.int32, (tq - row0, tk), 0)
            ki = lax.broadcasted_iota(jnp.int32, (tq - row0, tk), 1)
            keep = ki < qi
            u = jnp.where(jnp.concatenate([keep, keep], axis=1), u, 0.0)
        hi, lo = _split2(u)
        probs = []
        for h in range(2):
            sl = slice(h * tk, (h + 1) * tk)
            r = _dot(jnp.concatenate([hi[:, sl], lo[:, sl]], axis=1), tri)
            a = jnp.exp2(zl[:, sl] + r[:, :tk] + c_ref[h, rows, :])
            if diagonal:
                a = jnp.where(keep, a, 0.0)
            probs.append(a.astype(BF16))
            c_ref[h, rows, :] += r[:, tk:]
        return jnp.concatenate(probs, axis=1)

    def accumulate(kb, probs, row0):
        v2 = v_ref[pl.ds(pl.multiple_of(kb * tk, tk), tk), :]
        acc_ref[row0:, :] += _dot(probs, _head_split(v2, first))

    n_bulk = i * diag_blocks
    for d in reversed(range(diag_blocks)):
        accumulate(n_bulk + d, weights(scores(n_bulk + d, d * tk), d * tk, True), d * tk)

    qk_ref[0] = scores(jnp.maximum(n_bulk - 1, 0), 0)
    p_ref[1] = jnp.zeros((tq, 2 * tk), BF16)

    def body(step, carry):
        kb = n_bulk - 1 - diag_blocks * step
        for j in range(diag_blocks):
            cur = j % 2
            qk_ref[1 - cur] = scores(jnp.maximum(kb - j - 1, 0), 0)
            accumulate(kb - j + 1, p_ref[1 - cur], 0)
            p_ref[cur] = weights(qk_ref[cur], 0, False)
        return carry

    lax.fori_loop(0, i, body, 0)
    accumulate(0, p_ref[1], 0)
    o_ref[...] = acc_ref[...].astype(o_ref.dtype)


def _sb_prompt(sq, skb, svb, bias, tq=512, tk=128):
    b, s, _ = sq.shape
    tq = math.gcd(s, tq)
    assert tq % (2 * tk) == 0, "the ping-pong pipeline walks key blocks in pairs"
    bias_pairs = jnp.repeat(bias.astype(F32) * LOG2E, tk).reshape(PAIRS, 1, 2 * tk)
    kv = pl.BlockSpec((None, s, LANES), lambda bi, p, i: (bi, 0, p))
    qo = pl.BlockSpec((None, tq, LANES), lambda bi, p, i: (bi, i, p))
    return pl.pallas_call(
        functools.partial(_sb_prompt_kernel, tq=tq, tk=tk),
        grid=(b, PAIRS, s // tq),
        in_specs=[pl.BlockSpec((None, 1, 2 * tk), lambda bi, p, i: (p, 0, 0)),
                  _const_spec((2 * tk, 2 * tk)), qo, kv, kv],
        out_specs=qo,
        out_shape=jax.ShapeDtypeStruct((b, s, D_SB), BF16),
        scratch_shapes=[pltpu.VMEM((tq, LANES), F32), pltpu.VMEM((2, tq, tk), F32),
                        pltpu.VMEM((2, tq, 2 * tk), F32), pltpu.VMEM((2, tq, 2 * tk), BF16),
                        pltpu.VMEM((tq, 2 * LANES), BF16)],
        compiler_params=pltpu.CompilerParams(dimension_semantics=("parallel", "parallel", "arbitrary"),
                                             vmem_limit_bytes=VMEM_LIMIT),
        name="sb_prompt",
    )(bias_pairs, _suffix_matrix(tk), sq, skb, svb)


def _ret_sample_kernel(q_ref, k_ref, kt_ref, v_ref, s_ref, dec_ref, cross_ref, kdec_ref, sdec_ref, o_ref, so_ref):
    q = q_ref[...]
    v = v_ref[...]
    s = s_ref[...]
    bdot = lambda a, b_, dims: lax.dot_general(a, b_, (dims, ((0,), (0,))), preferred_element_type=F32)
    qk = bdot(q, k_ref[...].astype(BF16), ((2,), (2,)))
    prob = (qk * dec_ref[...]).astype(BF16)
    o = bdot(prob, v, ((2,), (1,)))
    o = o + bdot(q, s.astype(BF16), ((2,), (1,))) * cross_ref[...]
    o_ref[...] = o
    kd = (kt_ref[...] * kdec_ref[...]).astype(BF16)
    so_ref[...] = sdec_ref[...] * s + bdot(kd, v, ((2,), (1,)))


def _ret_sample(rq, rk, rv, state, lg, seq_blk=8, t_pad=16):
    bs = state.shape[0]
    t = rq.shape[0] // bs
    n = bs * H_RET
    nb = seq_blk * H_RET

    def heads(x):
        x = x.reshape(bs, t, H_RET, D_HEAD).transpose(0, 2, 1, 3).reshape(n, t, D_HEAD)
        return jnp.pad(x, ((0, 0), (0, t_pad - t), (0, 0)))

    decay, cross, kdec, sdec = _ret_tables(lg, t)
    pad_t = lambda x, axes: jnp.pad(x, [(0, t_pad - t) if a in axes else (0, 0) for a in range(x.ndim)])
    per_blk = lambda x, shape: jnp.tile(jnp.broadcast_to(x, (H_RET,) + shape), (seq_blk, 1, 1))
    dec_t = per_blk(pad_t(decay, (1, 2)), (t_pad, t_pad))
    cross_t = per_blk(pad_t(cross, (1,))[:, :, None], (t_pad, D_HEAD))
    kdec_t = per_blk(pad_t(kdec, (1,))[:, None, :], (D_HEAD, t_pad))
    sdec_t = per_blk(sdec[:, None, None], (D_HEAD, D_HEAD))
    k_heads = heads(rk)
    qkv = pl.BlockSpec((nb, t_pad, D_HEAD), lambda i: (i, 0, 0))
    ktr = pl.BlockSpec((nb, D_HEAD, t_pad), lambda i: (i, 0, 0))
    st = pl.BlockSpec((nb, D_HEAD, D_HEAD), lambda i: (i, 0, 0))
    o, s_new = pl.pallas_call(
        _ret_sample_kernel,
        grid=(bs // seq_blk,),
        in_specs=[qkv, qkv, ktr, qkv, st, _const_spec(dec_t.shape), _const_spec(cross_t.shape),
                  _const_spec(kdec_t.shape), _const_spec(sdec_t.shape)],
        out_specs=[qkv, st],
        out_shape=[jax.ShapeDtypeStruct((n, t_pad, D_HEAD), F32),
                   jax.ShapeDtypeStruct((n, D_HEAD, D_HEAD), F32)],
        compiler_params=pltpu.CompilerParams(dimension_semantics=("parallel",), vmem_limit_bytes=VMEM_LIMIT),
        name="ret_sample",
    )(heads(rq), k_heads, k_heads.transpose(0, 2, 1), heads(rv), state.reshape(n, D_HEAD, D_HEAD),
      dec_t, cross_t, kdec_t, sdec_t)
    o = o[:, :t].reshape(bs, H_RET, t, D_HEAD).transpose(0, 2, 1, 3).reshape(bs * t, D_RET)
    return o, s_new.reshape(bs, H_RET, D_HEAD, D_HEAD)


def _sb_sample_kernel(pt_ref, qbd_ref, bias_ref, tri_ref, kn_ref, vn_ref, *rest, pages_per_step, page, t):
    kp = rest[:pages_per_step]
    vp = rest[pages_per_step:2 * pages_per_step]
    o_ref, acc_ref, c_ref = rest[2 * pages_per_step:]
    step = pl.program_id(1)
    qbd = qbd_ref[...]
    bias = bias_ref[...]
    tri = tri_ref[...]
    nq = qbd.shape[0]

    def attend(qk, keep):
        n = qk.shape[1] // page
        blk = lambda x, g: x[:, g * page:(g + 1) * page]
        zl = qk * LOG2E + jnp.concatenate([bias] * n, axis=1)
        u = _neg_log2_survival(zl)
        if keep is not None:
            u = jnp.where(keep, u, 0.0)
        hi, lo = _split2(u)
        lhs = jnp.concatenate([jnp.concatenate([blk(hi, g), blk(lo, g)], axis=1) for g in range(n)], axis=0)
        r = _dot(lhs, tri)
        c = c_ref[...]
        probs = []
        for g in range(n):
            rg = r[g * nq:(g + 1) * nq]
            a = jnp.exp2(blk(zl, g) + rg[:, :page] + c)
            if keep is not None:
                a = jnp.where(keep, a, 0.0)
            probs.append(a.astype(BF16))
            c = c + rg[:, page:]
        c_ref[...] = c
        return jnp.concatenate(probs, axis=1)

    @pl.when(step == 0)
    def _():
        acc_ref[...] = jnp.zeros_like(acc_ref)
        c_ref[...] = jnp.zeros_like(c_ref)
        pad = jnp.zeros((page - kn_ref.shape[0], D_SB), F32)
        kn = jnp.concatenate([kn_ref[...], pad], axis=0).astype(BF16)
        vn = jnp.concatenate([vn_ref[...], pad], axis=0).astype(BF16)
        qt = lax.broadcasted_iota(jnp.int32, (nq, page), 0) // H_SB
        key = lax.broadcasted_iota(jnp.int32, (nq, page), 1)
        acc_ref[...] += _dot(attend(_dot_nt(qbd, kn), key < qt), vn)

    kcat = jnp.concatenate([kp[g][...].astype(BF16) for g in range(pages_per_step)], axis=1)
    vcat = jnp.concatenate([vp[g][...].astype(BF16) for g in range(pages_per_step)], axis=1)
    acc_ref[...] += _dot_nt(attend(_dot(qbd, kcat), None), vcat)

    @pl.when(step == pl.num_programs(1) - 1)
    def _():
        acc = acc_ref[...]
        row_head = lax.broadcasted_iota(jnp.int32, acc.shape, 0) % H_SB
        col_head = lax.broadcasted_iota(jnp.int32, acc.shape, 1) // D_HEAD
        own = jnp.where(row_head == col_head, acc, 0.0)
        o_ref[...] = jnp.sum(own.reshape(t, H_SB, D_SB), axis=1).astype(o_ref.dtype)


def _sb_sample(sq, sk, sv, cache_k, cache_v, page_table, bias, pages_per_step=16, t_pad=8):
    bs, n_pages = page_table.shape
    pages_per_step = math.gcd(n_pages, pages_per_step)
    t = sq.shape[0] // bs
    n_phys, page = cache_k.shape[:2]
    nq = t * H_SB
    eye = jnp.eye(H_SB, dtype=sq.dtype)
    qbd = (sq.reshape(bs, t, 1, H_SB, D_HEAD) * eye[None, None, :, :, None]).reshape(bs, nq, D_SB)
    bias_rows = jnp.broadcast_to(jnp.tile(bias.astype(F32) * LOG2E, t)[:, None], (nq, page))
    new = lambda x: jnp.pad(x.reshape(bs, t, D_SB), ((0, 0), (0, t_pad - t), (0, 0)))
    n_steps = n_pages // pages_per_step

    def page_spec(g):
        return pl.BlockSpec((None, D_SB, page),
                            lambda b, s, pt: (pt[b, n_pages - 1 - (s * pages_per_step + g)], 0, 0))

    per_seq = lambda shape: pl.BlockSpec((None,) + shape, lambda b, s, pt: (b, 0, 0))
    const = lambda shape: pl.BlockSpec(shape, lambda b, s, pt: (0,) * len(shape))
    ck = cache_k.transpose(0, 2, 3, 1).reshape(n_phys, D_SB, page)
    cv = cache_v.transpose(0, 2, 3, 1).reshape(n_phys, D_SB, page)
    grid_spec = pltpu.PrefetchScalarGridSpec(
        num_scalar_prefetch=1,
        grid=(bs, n_steps),
        in_specs=[per_seq((nq, D_SB)), const((nq, page)), const((2 * page, 2 * page)),
                  per_seq((t_pad, D_SB)), per_seq((t_pad, D_SB))]
                 + [page_spec(g) for g in range(pages_per_step)] * 2,
        out_specs=per_seq((t, D_SB)),
        scratch_shapes=[pltpu.VMEM((nq, D_SB), F32), pltpu.VMEM((nq, page), F32)],
    )
    o = pl.pallas_call(
        functools.partial(_sb_sample_kernel, pages_per_step=pages_per_step, page=page, t=t),
        grid_spec=grid_spec,
        out_shape=jax.ShapeDtypeStruct((bs, t, D_SB), BF16),
        compiler_params=pltpu.CompilerParams(dimension_semantics=("parallel", "arbitrary"),
                                             vmem_limit_bytes=VMEM_LIMIT),
        name="sb_sample",
    )(page_table, qbd, bias_rows, _suffix_matrix(page), new(sk), new(sv),
      *([ck] * pages_per_step), *([cv] * pages_per_step))
    return o.reshape(bs * t, D_SB)


def _merge_ffn_kernel(x_ref, oret_ref, g_ref, osb_ref, gm_ref, rnw_ref, wo_ref, n2w_ref, wg_ref, wu_ref, wd_ref,
                      fnw_ref, y_ref, *, ff_chunks):
    gm = gm_ref[...]

    def head_mean(v):
        hi, mid, lo = _split3(v)
        return _dot(hi, gm) + _dot(mid, gm) + _dot(lo, gm)

    o = oret_ref[...]
    d = o - head_mean(o)
    n = d * lax.rsqrt(head_mean(d * d) + EPS) * rnw_ref[...]
    g = g_ref[...]
    r = g * (1.0 / (1.0 + jnp.exp(-g))) * n
    x = x_ref[...]
    x = x + _dot(r.astype(BF16), wo_ref[:D_RET, :]) + _dot(osb_ref[...], wo_ref[D_RET:, :])
    h = _rms_scale(x, n2w_ref[...]).astype(BF16)
    d_ff = wg_ref.shape[1]
    fc = d_ff // ff_chunks
    down = None
    for c in range(ff_chunks):
        sl = slice(c * fc, (c + 1) * fc)
        gate = _dot(h, wg_ref[:, sl])
        act = gate * (1.0 / (1.0 + jnp.exp(-gate))) * _dot(h, wu_ref[:, sl])
        part = _dot(act.astype(BF16), wd_ref[sl, :])
        down = part if down is None else down + part
    x = x + down
    y_ref[...] = _rms_scale(x, fnw_ref[...])


def _merge_ffn(x, o_ret, g, o_sb, ret_norm_w, w_out, norm2_w, w_gate, w_up, w_down, final_norm_w, tm):
    n, dm = x.shape
    d_ff = w_gate.shape[1]
    head = np.arange(D_RET) // D_HEAD
    gm = jnp.asarray((head[:, None] == head[None, :]) / D_HEAD, dtype=BF16)
    row = lambda width: pl.BlockSpec((tm, width), lambda i: (i, 0))
    vec = lambda w: w.reshape(1, -1).astype(F32)
    return pl.pallas_call(
        functools.partial(_merge_ffn_kernel, ff_chunks=2),
        grid=(n // tm,),
        in_specs=[row(dm), row(D_RET), row(D_RET), row(D_SB), _const_spec((D_RET, D_RET)), _const_spec((1, D_RET)),
                  _const_spec(w_out.shape), _const_spec((1, dm)), _const_spec(w_gate.shape),
                  _const_spec(w_up.shape), _const_spec(w_down.shape), _const_spec((1, dm))],
        out_specs=row(dm),
        out_shape=jax.ShapeDtypeStruct((n, dm), F32),
        compiler_params=pltpu.CompilerParams(dimension_semantics=("parallel",), vmem_limit_bytes=VMEM_LIMIT),
        name="merge_ffn",
    )(x, o_ret, g, o_sb, gm, vec(ret_norm_w), w_out, vec(norm2_w), w_gate, w_up, w_down, vec(final_norm_w))


def _row_tile(n, want):
    return math.gcd(n, want)


def kernel(x_prompt, x_sample, cache_k, cache_v, state_ret, page_table, norm1_w, w_in, sb_bias, ret_norm_w, w_out,
           norm2_w, w_gate, w_up, w_down, final_norm_w):
    depth = w_in.shape[0]
    assert depth == 1, "single-layer step"
    bp, s, dm = x_prompt.shape
    bs, t, _ = x_sample.shape
    n_pages = page_table.shape[1]
    past_len = n_pages * cache_k.shape[2]
    lg = jnp.log1p(-jnp.exp2(-5.0 - jnp.arange(H_RET, dtype=F32)))
    l = 0
    w_in_b, w_out_b = w_in[l].astype(BF16), w_out[l].astype(BF16)
    w_gate_b, w_up_b, w_down_b = w_gate[l].astype(BF16), w_up[l].astype(BF16), w_down[l].astype(BF16)
    tail = (ret_norm_w[l], w_out_b, norm2_w[l], w_gate_b, w_up_b, w_down_b, final_norm_w)

    xp = x_prompt.reshape(bp * s, dm)
    tm = _row_tile(s, 512)
    rq, rk, rv, rg, sq, sk_t, sv_t, skb, svb = _in_proj(xp, norm1_w[l], w_in_b, jnp.arange(s), tm,
                                                        kv_position_minor=True)
    seq = lambda a: a.reshape(bp, s, -1)
    o_ret, s_p = _ret_prompt(seq(rq), seq(rk), seq(rv), lg)
    o_sb = _sb_prompt(seq(sq), seq(skb), seq(svb), sb_bias[l])
    y_prompt = _merge_ffn(xp, o_ret.reshape(bp * s, D_RET), rg, o_sb.reshape(bp * s, D_SB), *tail, tm=tm)

    xs = x_sample.reshape(bs * t, dm)
    tms = _row_tile(bs * t, 512)
    pos_s = jnp.tile(past_len + jnp.arange(t), tms // t)
    rq_s, rk_s, rv_s, rg_s, sq_s, sk_s, sv_s, _, _ = _in_proj(xs, norm1_w[l], w_in_b, pos_s, tms)
    o_ret_s, s_s = _ret_sample(rq_s, rk_s, rv_s, state_ret[l].astype(F32), lg)
    o_sb_s = _sb_sample(sq_s, sk_s, sv_s, cache_k[l], cache_v[l], page_table, sb_bias[l])
    y_sample = _merge_ffn(xs, o_ret_s, rg_s, o_sb_s, *tail, tm=tms)

    kv = lambda a, b_, n: a.reshape(1, b_, n, H_SB, D_HEAD)
    kv_t = lambda a: a.reshape(1, bp, H_SB, D_HEAD, s).transpose(0, 1, 4, 2, 3)
    return (y_prompt.reshape(bp, s, dm), y_sample.reshape(bs, t, dm),
            kv_t(sk_t).astype(cache_k.dtype), kv_t(sv_t).astype(cache_v.dtype),
            s_p[None].astype(state_ret.dtype),
            kv(sk_s, bs, t).astype(cache_k.dtype), kv(sv_s, bs, t).astype(cache_v.dtype),
            s_s[None].astype(state_ret.dtype))
```

```python
import functools
import math

import jax
import jax.numpy as jnp
import numpy as np
from jax import lax
from jax.experimental import pallas as pl
from jax.experimental.pallas import tpu as pltpu

D_HEAD = 64
H_RET = 8
H_SB = 8
D_RET = H_RET * D_HEAD
D_SB = H_SB * D_HEAD
RET_CHUNK = 128
ROPE_BASE = 10000.0
EPS = 1e-6
LANES = 128
PAIRS = D_SB // LANES
VMEM_LIMIT = 56 * 1024 * 1024
LOG2E = 1.4426950408889634

F32 = jnp.float32
BF16 = jnp.bfloat16


def _dot(a, b):
    return jnp.dot(a, b, preferred_element_type=F32)


def _dot_nt(a, b):
    return lax.dot_general(a, b, (((1,), (1,)), ((), ())), preferred_element_type=F32)


def _dot_tn(a, b):
    return lax.dot_general(a, b, (((0,), (0,)), ((), ())), preferred_element_type=F32)


def _split2(x):
    hi = x.astype(BF16)
    lo = (x - hi.astype(F32)).astype(BF16)
    return hi, lo


def _split3(x):
    hi = x.astype(BF16)
    r = x - hi.astype(F32)
    mid = r.astype(BF16)
    lo = (r - mid.astype(F32)).astype(BF16)
    return hi, mid, lo


def _head_split(x2, lane_is_first):
    zero = jnp.zeros_like(x2)
    return jnp.concatenate([jnp.where(lane_is_first, x2, zero), jnp.where(lane_is_first, zero, x2)], axis=0)


def _neg_log2_survival(zl):
    sign_bit = jnp.int32(-2 ** 31)
    neg_abs = pltpu.bitcast(pltpu.bitcast(zl, jnp.int32) | sign_bit, F32)
    return jnp.maximum(zl, 0.0) + jnp.log2(1.0 + jnp.exp2(neg_abs))


def _rms_scale(x, w):
    ms = jnp.mean(x * x, axis=-1, keepdims=True)
    return x * lax.rsqrt(ms + EPS) * w


def _const_spec(shape):
    nd = len(shape)
    return pl.BlockSpec(shape, lambda *_: (0,) * nd, pipeline_mode=pl.Buffered(1))


def _in_proj_kernel(x_ref, nw_ref, w_ref, cos_ref, sin_ref,
                    rq_ref, rk_ref, rv_ref, rg_ref, sq_ref, sk_ref, sv_ref, skb_ref, svb_ref, *, kv_position_minor):
    h = _rms_scale(x_ref[...], nw_ref[...]).astype(BF16)
    cos = cos_ref[...]
    sin = sin_ref[...]
    first_half = (lax.broadcasted_iota(jnp.int32, (1, LANES), 1) % D_HEAD) < (D_HEAD // 2)

    def group(g):
        return _dot(h, w_ref[:, g * D_RET:(g + 1) * D_RET])

    def rope(z):
        outs = []
        for c in range(D_RET // LANES):
            zc = z[:, c * LANES:(c + 1) * LANES]
            swapped = jnp.where(first_half,
                                pltpu.roll(zc, LANES - D_HEAD // 2, axis=1),
                                pltpu.roll(zc, D_HEAD // 2, axis=1))
            outs.append(zc * cos + swapped * sin)
        return jnp.concatenate(outs, axis=1)

    scale = D_HEAD ** -0.5
    rq_ref[...] = rope(group(0)).astype(BF16)
    rk_ref[...] = rope(group(1)) * scale
    rv_ref[...] = group(2).astype(BF16)
    rg_ref[...] = group(3)
    sq_ref[...] = (group(4) * scale).astype(BF16)
    sk = group(5)
    sk_ref[...] = sk.T if kv_position_minor else sk
    skb_ref[...] = sk.astype(BF16)
    sv = group(6)
    sv_ref[...] = sv.T if kv_position_minor else sv
    svb_ref[...] = sv.astype(BF16)


def _rope_tables(pos):
    half = D_HEAD // 2
    inv = ROPE_BASE ** (-(jnp.arange(half, dtype=F32) / half))
    ang = pos.astype(F32)[:, None] * inv[None, :]
    cos, sin = jnp.cos(ang), jnp.sin(ang)
    reps = LANES // D_HEAD
    return (jnp.tile(jnp.concatenate([cos, cos], axis=1), (1, reps)),
            jnp.tile(jnp.concatenate([-sin, sin], axis=1), (1, reps)))


def _in_proj(x, norm_w, w_bf16, pos, tm, kv_position_minor=False):
    n, dm = x.shape
    cos, sin = _rope_tables(pos)
    pos_tiles = pos.shape[0] // tm
    row = lambda i: (i, 0)
    tab = lambda i: (i % pos_tiles, 0)
    f32_out = jax.ShapeDtypeStruct((n, D_RET), F32)
    bf_out = jax.ShapeDtypeStruct((n, D_RET), BF16)
    out_spec = pl.BlockSpec((tm, D_RET), row)
    kv_out, kv_spec = f32_out, out_spec
    if kv_position_minor:
        kv_out = jax.ShapeDtypeStruct((n // pos.shape[0], D_SB, pos.shape[0]), F32)
        kv_spec = pl.BlockSpec((None, D_SB, tm), lambda i: (i // pos_tiles, 0, i % pos_tiles))
    return pl.pallas_call(
        functools.partial(_in_proj_kernel, kv_position_minor=kv_position_minor),
        grid=(n // tm,),
        in_specs=[pl.BlockSpec((tm, dm), row), _const_spec((1, dm)), _const_spec(w_bf16.shape),
                  pl.BlockSpec((tm, LANES), tab), pl.BlockSpec((tm, LANES), tab)],
        out_specs=[out_spec] * 5 + [kv_spec] * 2 + [out_spec] * 2,
        out_shape=[bf_out, f32_out, bf_out, f32_out, bf_out, kv_out, kv_out, bf_out, bf_out],
        compiler_params=pltpu.CompilerParams(dimension_semantics=("parallel",), vmem_limit_bytes=VMEM_LIMIT),
        name="in_proj",
    )(x, norm_w.reshape(1, dm), w_bf16, cos, sin)


def _ret_tables(lg, c):
    idx = jnp.arange(c, dtype=F32)
    diff = idx[:, None] - idx[None, :]
    decay = jnp.where(diff >= 0, jnp.exp(lg[:, None, None] * jnp.maximum(diff, 0.0)), 0.0)
    cross = jnp.exp(lg[:, None] * (idx[None, :] + 1.0))
    kdec = jnp.exp(lg[:, None] * (c - 1.0 - idx[None, :]))
    sdec = jnp.exp(lg * c)
    return decay, cross, kdec, sdec


def _ret_prompt_kernel(q_ref, k_ref, v_ref, dec_ref, cross_ref, kdec_ref, sdec_ref, o_ref, s_ref, s_scr, *, chunks):
    step = pl.program_id(1)
    c = q_ref.shape[0] // chunks

    @pl.when(step == 0)
    def _():
        s_scr[...] = jnp.zeros_like(s_scr)

    lane = lax.broadcasted_iota(jnp.int32, (1, LANES), 1)
    first = lane < D_HEAD
    row_first = lax.broadcasted_iota(jnp.int32, (LANES, 1), 0) < D_HEAD
    same_head = row_first == first

    for p in range(PAIRS):
        sl = slice(p * LANES, (p + 1) * LANES)
        s = s_scr[p]
        for j in range(chunks):
            rows = slice(j * c, (j + 1) * c)
            q2 = q_ref[rows, sl]
            k2f = k_ref[rows, sl]
            v2 = v_ref[rows, sl]
            qk = _dot_nt(q2, _head_split(k2f.astype(BF16), first))
            prob = (qk * dec_ref[p]).astype(BF16)
            o = _dot(prob, _head_split(v2, first))
            o_ref[rows, sl] = o + _dot(q2, s.astype(BF16)) * cross_ref[:, sl]
            kd = (k2f * kdec_ref[:, sl]).astype(BF16)
            upd = _dot_tn(kd, v2)
            s = sdec_ref[p] * s + jnp.where(same_head, upd, 0.0)
        s_scr[p] = s

    @pl.when(step == pl.num_programs(1) - 1)
    def _():
        s_ref[...] = s_scr[...]


def _ret_prompt(rq, rk, rv, lg):
    b, s, _ = rq.shape
    c = RET_CHUNK
    decay, cross, kdec, sdec = _ret_tables(lg, c)
    dec_pairs = decay.reshape(PAIRS, 2, c, c).transpose(0, 2, 1, 3).reshape(PAIRS, c, 2 * c)
    lanes = lambda t: jnp.repeat(t.T, D_HEAD, axis=1)
    sdec_pairs = jnp.broadcast_to(jnp.repeat(sdec, D_HEAD).reshape(PAIRS, LANES, 1), (PAIRS, LANES, LANES))
    chunks = math.gcd(s // c, 4)
    blk = pl.BlockSpec((None, chunks * c, D_RET), lambda bi, ci: (bi, ci, 0))
    o, st = pl.pallas_call(
        functools.partial(_ret_prompt_kernel, chunks=chunks),
        grid=(b, s // (chunks * c)),
        in_specs=[blk, blk, blk, _const_spec((PAIRS, c, 2 * c)), _const_spec((c, D_RET)),
                  _const_spec((c, D_RET)), _const_spec((PAIRS, LANES, LANES))],
        out_specs=[blk, pl.BlockSpec((None, PAIRS, LANES, LANES), lambda bi, ci: (bi, 0, 0, 0))],
        out_shape=[jax.ShapeDtypeStruct((b, s, D_RET), F32),
                   jax.ShapeDtypeStruct((b, PAIRS, LANES, LANES), F32)],
        scratch_shapes=[pltpu.VMEM((PAIRS, LANES, LANES), F32)],
        compiler_params=pltpu.CompilerParams(dimension_semantics=("parallel", "arbitrary"),
                                             vmem_limit_bytes=VMEM_LIMIT),
        name="ret_prompt",
    )(rq, rk, rv, dec_pairs, lanes(cross), lanes(kdec), sdec_pairs)
    st = st.reshape(b, PAIRS, 2, D_HEAD, 2, D_HEAD)
    st = jnp.stack([st[:, :, 0, :, 0, :], st[:, :, 1, :, 1, :]], axis=2).reshape(b, H_RET, D_HEAD, D_HEAD)
    return o, st


def _suffix_matrix(n):
    j = np.arange(2 * n)[:, None] % n
    s = np.arange(2 * n)[None, :]
    return jnp.asarray(-np.where(s < n, j >= s, True).astype(np.float32), dtype=BF16)


def _sb_prompt_kernel(bias_ref, tri_ref, q_ref, k_ref, v_ref, o_ref, acc_ref, c_ref, qk_ref, p_ref, qs_ref, *,
                      tq, tk):
    i = pl.program_id(2)
    first = lax.broadcasted_iota(jnp.int32, (1, LANES), 1) < D_HEAD
    bias = bias_ref[...]
    tri = tri_ref[...]
    acc_ref[...] = jnp.zeros_like(acc_ref)
    c_ref[...] = jnp.zeros_like(c_ref)
    diag_blocks = tq // tk
    qs_ref[...] = jnp.concatenate(_split2(q_ref[...].astype(F32) * LOG2E), axis=1)

    def scores(kb, row0):
        kz = _head_split(k_ref[pl.ds(pl.multiple_of(kb * tk, tk), tk), :], first)
        return _dot_nt(qs_ref[row0:, :], jnp.concatenate([kz, kz], axis=1))

    def weights(qk, row0, diagonal):
        rows = slice(row0, tq)
        zl = qk + bias
        u = _neg_log2_survival(zl)
        if diagonal:
            qi = lax.broadcasted_iota(jnp.int32, (tq - row0, tk), 0)
            ki = lax.broadcasted_iota(jnp.int32, (tq - row0, tk), 1)
            keep = ki < qi
            u = jnp.where(jnp.concatenate([keep, keep], axis=1), u, 0.0)
        hi, lo = _split2(u)
        probs = []
        for h in range(2):
            sl = slice(h * tk, (h + 1) * tk)
            r = _dot(jnp.concatenate([hi[:, sl], lo[:, sl]], axis=1), tri)
            a = jnp.exp2(zl[:, sl] + r[:, :tk] + c_ref[h, rows, :])
            if diagonal:
                a = jnp.where(keep, a, 0.0)
            probs.append(a.astype(BF16))
            c_ref[h, rows, :] += r[:, tk:]
        return jnp.concatenate(probs, axis=1)

    def accumulate(kb, probs, row0):
        v2 = v_ref[pl.ds(pl.multiple_of(kb * tk, tk), tk), :]
        acc_ref[row0:, :] += _dot(probs, _head_split(v2, first))

    n_bulk = i * diag_blocks
    qk = scores(n_bulk + diag_blocks - 1, (diag_blocks - 1) * tk)
    probs = None
    for d in reversed(range(diag_blocks)):
        qk_next = scores(n_bulk + d - 1, (d - 1) * tk) if d else scores(jnp.maximum(n_bulk - 1, 0), 0)
        if probs is not None:
            accumulate(n_bulk + d + 1, probs, (d + 1) * tk)
        probs = weights(qk, d * tk, True)
        qk = qk_next
    qk_ref[0] = qk
    p_ref[1] = probs

    def walk(first_kb, blocks):
        for j in range(blocks):
            cur = j % 2
            qk_ref[1 - cur] = scores(jnp.maximum(first_kb - j - 1, 0), 0)
            accumulate(first_kb - j + 1, p_ref[1 - cur], 0)
            p_ref[cur] = weights(qk_ref[cur], 0, False)

    long_trip = 2 * diag_blocks

    def long_body(step, carry):
        walk(n_bulk - 1 - long_trip * step, long_trip)
        return carry

    def short_body(step, carry):
        walk(n_bulk - 1 - long_trip * (i // 2), diag_blocks)
        return carry

    lax.fori_loop(0, i // 2, long_body, 0)
    lax.fori_loop(0, i % 2, short_body, 0)
    accumulate(0, p_ref[1], 0)
    o_ref[...] = acc_ref[...].astype(o_ref.dtype)


def _sb_prompt(sq, skb, svb, bias, tq=512, tk=128):
    b, s, _ = sq.shape
    tq = math.gcd(s, tq)
    assert tq % (2 * tk) == 0, "the ping-pong pipeline walks key blocks in pairs"
    bias_pairs = jnp.repeat(bias.astype(F32) * LOG2E, tk).reshape(PAIRS, 1, 2 * tk)
    kv = pl.BlockSpec((None, s, LANES), lambda bi, p, i: (bi, 0, p))
    qo = pl.BlockSpec((None, tq, LANES), lambda bi, p, i: (bi, i, p))
    return pl.pallas_call(
        functools.partial(_sb_prompt_kernel, tq=tq, tk=tk),
        grid=(b, PAIRS, s // tq),
        in_specs=[pl.BlockSpec((None, 1, 2 * tk), lambda bi, p, i: (p, 0, 0)),
                  _const_spec((2 * tk, 2 * tk)), qo, kv, kv],
        out_specs=qo,
        out_shape=jax.ShapeDtypeStruct((b, s, D_SB), BF16),
        scratch_shapes=[pltpu.VMEM((tq, LANES), F32), pltpu.VMEM((2, tq, tk), F32),
                        pltpu.VMEM((2, tq, 2 * tk), F32), pltpu.VMEM((2, tq, 2 * tk), BF16),
                        pltpu.VMEM((tq, 2 * LANES), BF16)],
        compiler_params=pltpu.CompilerParams(dimension_semantics=("parallel", "parallel", "arbitrary"),
                                             vmem_limit_bytes=VMEM_LIMIT),
        name="sb_prompt",
    )(bias_pairs, _suffix_matrix(tk), sq, skb, svb)


def _ret_sample_kernel(q_ref, k_ref, kt_ref, v_ref, s_ref, dec_ref, cross_ref, kdec_ref, sdec_ref, o_ref, so_ref):
    q = q_ref[...]
    v = v_ref[...]
    s = s_ref[...]
    bdot = lambda a, b_, dims: lax.dot_general(a, b_, (dims, ((0,), (0,))), preferred_element_type=F32)
    qk = bdot(q, k_ref[...].astype(BF16), ((2,), (2,)))
    prob = (qk * dec_ref[...]).astype(BF16)
    o = bdot(prob, v, ((2,), (1,)))
    o = o + bdot(q, s.astype(BF16), ((2,), (1,))) * cross_ref[...]
    o_ref[...] = o
    kd = (kt_ref[...] * kdec_ref[...]).astype(BF16)
    so_ref[...] = sdec_ref[...] * s + bdot(kd, v, ((2,), (1,)))


def _ret_sample(rq, rk, rv, state, lg, seq_blk=8, t_pad=16):
    bs = state.shape[0]
    t = rq.shape[0] // bs
    n = bs * H_RET
    nb = seq_blk * H_RET

    def heads(x):
        x = x.reshape(bs, t, H_RET, D_HEAD).transpose(0, 2, 1, 3).reshape(n, t, D_HEAD)
        return jnp.pad(x, ((0, 0), (0, t_pad - t), (0, 0)))

    decay, cross, kdec, sdec = _ret_tables(lg, t)
    pad_t = lambda x, axes: jnp.pad(x, [(0, t_pad - t) if a in axes else (0, 0) for a in range(x.ndim)])
    per_blk = lambda x, shape: jnp.tile(jnp.broadcast_to(x, (H_RET,) + shape), (seq_blk, 1, 1))
    dec_t = per_blk(pad_t(decay, (1, 2)), (t_pad, t_pad))
    cross_t = per_blk(pad_t(cross, (1,))[:, :, None], (t_pad, D_HEAD))
    kdec_t = per_blk(pad_t(kdec, (1,))[:, None, :], (D_HEAD, t_pad))
    sdec_t = per_blk(sdec[:, None, None], (D_HEAD, D_HEAD))
    k_heads = heads(rk)
    qkv = pl.BlockSpec((nb, t_pad, D_HEAD), lambda i: (i, 0, 0))
    ktr = pl.BlockSpec((nb, D_HEAD, t_pad), lambda i: (i, 0, 0))
    st = pl.BlockSpec((nb, D_HEAD, D_HEAD), lambda i: (i, 0, 0))
    o, s_new = pl.pallas_call(
        _ret_sample_kernel,
        grid=(bs // seq_blk,),
        in_specs=[qkv, qkv, ktr, qkv, st, _const_spec(dec_t.shape), _const_spec(cross_t.shape),
                  _const_spec(kdec_t.shape), _const_spec(sdec_t.shape)],
        out_specs=[qkv, st],
        out_shape=[jax.ShapeDtypeStruct((n, t_pad, D_HEAD), F32),
                   jax.ShapeDtypeStruct((n, D_HEAD, D_HEAD), F32)],
        compiler_params=pltpu.CompilerParams(dimension_semantics=("parallel",), vmem_limit_bytes=VMEM_LIMIT),
        name="ret_sample",
    )(heads(rq), k_heads, k_heads.transpose(0, 2, 1), heads(rv), state.reshape(n, D_HEAD, D_HEAD),
      dec_t, cross_t, kdec_t, sdec_t)
    o = o[:, :t].reshape(bs, H_RET, t, D_HEAD).transpose(0, 2, 1, 3).reshape(bs * t, D_RET)
    return o, s_new.reshape(bs, H_RET, D_HEAD, D_HEAD)


def _sb_sample_kernel(pt_ref, qbd_ref, bias_ref, tri_ref, kn_ref, vn_ref, *rest, pages_per_step, page, t):
    kp = rest[:pages_per_step]
    vp = rest[pages_per_step:2 * pages_per_step]
    o_ref, acc_ref, c_ref = rest[2 * pages_per_step:]
    step = pl.program_id(1)
    qbd = qbd_ref[...]
    bias = bias_ref[...]
    tri = tri_ref[...]
    nq = qbd.shape[0]

    def attend(qk, keep):
        n = qk.shape[1] // page
        blk = lambda x, g: x[:, g * page:(g + 1) * page]
        zl = qk * LOG2E + jnp.concatenate([bias] * n, axis=1)
        u = _neg_log2_survival(zl)
        if keep is not None:
            u = jnp.where(keep, u, 0.0)
        hi, lo = _split2(u)
        lhs = jnp.concatenate([jnp.concatenate([blk(hi, g), blk(lo, g)], axis=1) for g in range(n)], axis=0)
        r = _dot(lhs, tri)
        c = c_ref[...]
        probs = []
        for g in range(n):
            rg = r[g * nq:(g + 1) * nq]
            a = jnp.exp2(blk(zl, g) + rg[:, :page] + c)
            if keep is not None:
                a = jnp.where(keep, a, 0.0)
            probs.append(a.astype(BF16))
            c = c + rg[:, page:]
        c_ref[...] = c
        return jnp.concatenate(probs, axis=1)

    @pl.when(step == 0)
    def _():
        acc_ref[...] = jnp.zeros_like(acc_ref)
        c_ref[...] = jnp.zeros_like(c_ref)
        pad = jnp.zeros((page - kn_ref.shape[0], D_SB), F32)
        kn = jnp.concatenate([kn_ref[...], pad], axis=0).astype(BF16)
        vn = jnp.concatenate([vn_ref[...], pad], axis=0).astype(BF16)
        qt = lax.broadcasted_iota(jnp.int32, (nq, page), 0) // H_SB
        key = lax.broadcasted_iota(jnp.int32, (nq, page), 1)
        acc_ref[...] += _dot(attend(_dot_nt(qbd, kn), key < qt), vn)

    kcat = jnp.concatenate([kp[g][...].astype(BF16) for g in range(pages_per_step)], axis=1)
    vcat = jnp.concatenate([vp[g][...].astype(BF16) for g in range(pages_per_step)], axis=1)
    acc_ref[...] += _dot_nt(attend(_dot(qbd, kcat), None), vcat)

    @pl.when(step == pl.num_programs(1) - 1)
    def _():
        acc = acc_ref[...]
        row_head = lax.broadcasted_iota(jnp.int32, acc.shape, 0) % H_SB
        col_head = lax.broadcasted_iota(jnp.int32, acc.shape, 1) // D_HEAD
        own = jnp.where(row_head == col_head, acc, 0.0)
        o_ref[...] = jnp.sum(own.reshape(t, H_SB, D_SB), axis=1).astype(o_ref.dtype)


def _sb_sample(sq, sk, sv, cache_k, cache_v, page_table, bias, pages_per_step=16, t_pad=8):
    bs, n_pages = page_table.shape
    pages_per_step = math.gcd(n_pages, pages_per_step)
    t = sq.shape[0] // bs
    n_phys, page = cache_k.shape[:2]
    nq = t * H_SB
    eye = jnp.eye(H_SB, dtype=sq.dtype)
    qbd = (sq.reshape(bs, t, 1, H_SB, D_HEAD) * eye[None, None, :, :, None]).reshape(bs, nq, D_SB)
    bias_rows = jnp.broadcast_to(jnp.tile(bias.astype(F32) * LOG2E, t)[:, None], (nq, page))
    new = lambda x: jnp.pad(x.reshape(bs, t, D_SB), ((0, 0), (0, t_pad - t), (0, 0)))
    n_steps = n_pages // pages_per_step

    def page_spec(g):
        return pl.BlockSpec((None, D_SB, page),
                            lambda b, s, pt: (pt[b, n_pages - 1 - (s * pages_per_step + g)], 0, 0))

    per_seq = lambda shape: pl.BlockSpec((None,) + shape, lambda b, s, pt: (b, 0, 0))
    const = lambda shape: pl.BlockSpec(shape, lambda b, s, pt: (0,) * len(shape))
    ck = cache_k.transpose(0, 2, 3, 1).reshape(n_phys, D_SB, page)
    cv = cache_v.transpose(0, 2, 3, 1).reshape(n_phys, D_SB, page)
    grid_spec = pltpu.PrefetchScalarGridSpec(
        num_scalar_prefetch=1,
        grid=(bs, n_steps),
        in_specs=[per_seq((nq, D_SB)), const((nq, page)), const((2 * page, 2 * page)),
                  per_seq((t_pad, D_SB)), per_seq((t_pad, D_SB))]
                 + [page_spec(g) for g in range(pages_per_step)] * 2,
        out_specs=per_seq((t, D_SB)),
        scratch_shapes=[pltpu.VMEM((nq, D_SB), F32), pltpu.VMEM((nq, page), F32)],
    )
    o = pl.pallas_call(
        functools.partial(_sb_sample_kernel, pages_per_step=pages_per_step, page=page, t=t),
        grid_spec=grid_spec,
        out_shape=jax.ShapeDtypeStruct((bs, t, D_SB), BF16),
        compiler_params=pltpu.CompilerParams(dimension_semantics=("parallel", "arbitrary"),
                                             vmem_limit_bytes=VMEM_LIMIT),
        name="sb_sample",
    )(page_table, qbd, bias_rows, _suffix_matrix(page), new(sk), new(sv),
      *([ck] * pages_per_step), *([cv] * pages_per_step))
    return o.reshape(bs * t, D_SB)


def _merge_ffn_kernel(x_ref, oret_ref, g_ref, osb_ref, gm_ref, rnw_ref, wo_ref, n2w_ref, wg_ref, wu_ref, wd_ref,
                      fnw_ref, y_ref, *, ff_chunks):
    gm = gm_ref[...]

    def head_mean(v):
        hi, mid, lo = _split3(v)
        return _dot(hi, gm) + _dot(mid, gm) + _dot(lo, gm)

    o = oret_ref[...]
    d = o - head_mean(o)
    n = d * lax.rsqrt(head_mean(d * d) + EPS) * rnw_ref[...]
    g = g_ref[...]
    r = g * (1.0 / (1.0 + jnp.exp(-g))) * n
    x = x_ref[...]
    x = x + _dot(r.astype(BF16), wo_ref[:D_RET, :]) + _dot(osb_ref[...], wo_ref[D_RET:, :])
    h = _rms_scale(x, n2w_ref[...]).astype(BF16)
    d_ff = wg_ref.shape[1]
    fc = d_ff // ff_chunks
    down = None
    for c in range(ff_chunks):
        sl = slice(c * fc, (c + 1) * fc)
        gate = _dot(h, wg_ref[:, sl])
        act = gate * (1.0 / (1.0 + jnp.exp(-gate))) * _dot(h, wu_ref[:, sl])
        part = _dot(act.astype(BF16), wd_ref[sl, :])
        down = part if down is None else down + part
    x = x + down
    y_ref[...] = _rms_scale(x, fnw_ref[...])


def _merge_ffn(x, o_ret, g, o_sb, ret_norm_w, w_out, norm2_w, w_gate, w_up, w_down, final_norm_w, tm):
    n, dm = x.shape
    d_ff = w_gate.shape[1]
    head = np.arange(D_RET) // D_HEAD
    gm = jnp.asarray((head[:, None] == head[None, :]) / D_HEAD, dtype=BF16)
    row = lambda width: pl.BlockSpec((tm, width), lambda i: (i, 0))
    vec = lambda w: w.reshape(1, -1).astype(F32)
    return pl.pallas_call(
        functools.partial(_merge_ffn_kernel, ff_chunks=2),
        grid=(n // tm,),
        in_specs=[row(dm), row(D_RET), row(D_RET), row(D_SB), _const_spec((D_RET, D_RET)), _const_spec((1, D_RET)),
                  _const_spec(w_out.shape), _const_spec((1, dm)), _const_spec(w_gate.shape),
                  _const_spec(w_up.shape), _const_spec(w_down.shape), _const_spec((1, dm))],
        out_specs=row(dm),
        out_shape=jax.ShapeDtypeStruct((n, dm), F32),
        compiler_params=pltpu.CompilerParams(dimension_semantics=("parallel",), vmem_limit_bytes=VMEM_LIMIT),
        name="merge_ffn",
    )(x, o_ret, g, o_sb, gm, vec(ret_norm_w), w_out, vec(norm2_w), w_gate, w_up, w_down, vec(final_norm_w))


def _row_tile(n, want):
    return math.gcd(n, want)


def kernel(x_prompt, x_sample, cache_k, cache_v, state_ret, page_table, norm1_w, w_in, sb_bias, ret_norm_w, w_out,
           norm2_w, w_gate, w_up, w_down, final_norm_w):
    depth = w_in.shape[0]
    assert depth == 1, "single-layer step"
    bp, s, dm = x_prompt.shape
    bs, t, _ = x_sample.shape
    n_pages = page_table.shape[1]
    past_len = n_pages * cache_k.shape[2]
    lg = jnp.log1p(-jnp.exp2(-5.0 - jnp.arange(H_RET, dtype=F32)))
    l = 0
    w_in_b, w_out_b = w_in[l].astype(BF16), w_out[l].astype(BF16)
    w_gate_b, w_up_b, w_down_b = w_gate[l].astype(BF16), w_up[l].astype(BF16), w_down[l].astype(BF16)
    tail = (ret_norm_w[l], w_out_b, norm2_w[l], w_gate_b, w_up_b, w_down_b, final_norm_w)

    xp = x_prompt.reshape(bp * s, dm)
    tm = _row_tile(s, 512)
    rq, rk, rv, rg, sq, sk_t, sv_t, skb, svb = _in_proj(xp, norm1_w[l], w_in_b, jnp.arange(s), tm,
                                                        kv_position_minor=True)
    seq = lambda a: a.reshape(bp, s, -1)
    o_ret, s_p = _ret_prompt(seq(rq), seq(rk), seq(rv), lg)
    o_sb = _sb_prompt(seq(sq), seq(skb), seq(svb), sb_bias[l])
    y_prompt = _merge_ffn(xp, o_ret.reshape(bp * s, D_RET), rg, o_sb.reshape(bp * s, D_SB), *tail, tm=tm)

    xs = x_sample.reshape(bs * t, dm)
    tms = _row_tile(bs * t, 512)
    pos_s = jnp.tile(past_len + jnp.arange(t), tms // t)
    rq_s, rk_s, rv_s, rg_s, sq_s, sk_s, sv_s, _, _ = _in_proj(xs, norm1_w[l], w_in_b, pos_s, tms)
    o_ret_s, s_s = _ret_sample(rq_s, rk_s, rv_s, state_ret[l].astype(F32), lg)
    o_sb_s = _sb_sample(sq_s, sk_s, sv_s, cache_k[l], cache_v[l], page_table, sb_bias[l])
    y_sample = _merge_ffn(xs, o_ret_s, rg_s, o_sb_s, *tail, tm=tms)

    kv = lambda a, b_, n: a.reshape(1, b_, n, H_SB, D_HEAD)
    kv_t = lambda a: a.reshape(1, bp, H_SB, D_HEAD, s).transpose(0, 1, 4, 2, 3)
    return (y_prompt.reshape(bp, s, dm), y_sample.reshape(bs, t, dm),
            kv_t(sk_t).astype(cache_k.dtype), kv_t(sv_t).astype(cache_v.dtype),
            s_p[None].astype(state_ret.dtype),
            kv(sk_s, bs, t).astype(cache_k.dtype), kv(sv_s, bs, t).astype(cache_v.dtype),
            s_s[None].astype(state_ret.dtype))
```

```python
import functools
import math

import jax
import jax.numpy as jnp
import numpy as np
from jax import lax
from jax.experimental import pallas as pl
from jax.experimental.pallas import tpu as pltpu

D_HEAD = 64
H_RET = 8
H_SB = 8
D_RET = H_RET * D_HEAD
D_SB = H_SB * D_HEAD
RET_CHUNK = 128
ROPE_BASE = 10000.0
EPS = 1e-6
LANES = 128
PAIRS = D_SB // LANES
VMEM_LIMIT = 56 * 1024 * 1024
LOG2E = 1.4426950408889634

F32 = jnp.float32
BF16 = jnp.bfloat16


def _dot(a, b):
    return jnp.dot(a, b, preferred_element_type=F32)


def _dot_nt(a, b):
    return lax.dot_general(a, b, (((1,), (1,)), ((), ())), preferred_element_type=F32)


def _dot_tn(a, b):
    return lax.dot_general(a, b, (((0,), (0,)), ((), ())), preferred_element_type=F32)


def _split2(x):
    hi = x.astype(BF16)
    lo = (x - hi.astype(F32)).astype(BF16)
    return hi, lo


def _split3(x):
    hi = x.astype(BF16)
    r = x - hi.astype(F32)
    mid = r.astype(BF16)
    lo = (r - mid.astype(F32)).astype(BF16)
    return hi, mid, lo


def _head_split(x2, lane_is_first):
    zero = jnp.zeros_like(x2)
    return jnp.concatenate([jnp.where(lane_is_first, x2, zero), jnp.where(lane_is_first, zero, x2)], axis=0)


def _neg_log2_survival(zl):
    sign_bit = jnp.int32(-2 ** 31)
    neg_abs = pltpu.bitcast(pltpu.bitcast(zl, jnp.int32) | sign_bit, F32)
    return jnp.maximum(zl, 0.0) + jnp.log2(1.0 + jnp.exp2(neg_abs))


def _rms_scale(x, w):
    ms = jnp.mean(x * x, axis=-1, keepdims=True)
    return x * lax.rsqrt(ms + EPS) * w


def _const_spec(shape):
    nd = len(shape)
    return pl.BlockSpec(shape, lambda *_: (0,) * nd, pipeline_mode=pl.Buffered(1))


def _in_proj_kernel(x_ref, nw_ref, w_ref, cos_ref, sin_ref,
                    rq_ref, rk_ref, rv_ref, rg_ref, sq_ref, sk_ref, sv_ref, skb_ref, svb_ref, *, kv_position_minor):
    h = _rms_scale(x_ref[...], nw_ref[...]).astype(BF16)
    cos = cos_ref[...]
    sin = sin_ref[...]
    first_half = (lax.broadcasted_iota(jnp.int32, (1, LANES), 1) % D_HEAD) < (D_HEAD // 2)

    def group(g):
        return _dot(h, w_ref[:, g * D_RET:(g + 1) * D_RET])

    def rope(z):
        outs = []
        for c in range(D_RET // LANES):
            zc = z[:, c * LANES:(c + 1) * LANES]
            swapped = jnp.where(first_half,
                                pltpu.roll(zc, LANES - D_HEAD // 2, axis=1),
                                pltpu.roll(zc, D_HEAD // 2, axis=1))
            outs.append(zc * cos + swapped * sin)
        return jnp.concatenate(outs, axis=1)

    scale = D_HEAD ** -0.5
    rq_ref[...] = rope(group(0)).astype(BF16)
    rk_ref[...] = rope(group(1)) * scale
    rv_ref[...] = group(2).astype(BF16)
    rg_ref[...] = group(3)
    sq_ref[...] = (group(4) * scale).astype(BF16)
    sk = group(5)
    sk_ref[...] = sk.T if kv_position_minor else sk
    skb_ref[...] = sk.astype(BF16)
    sv = group(6)
    sv_ref[...] = sv.T if kv_position_minor else sv
    svb_ref[...] = sv.astype(BF16)


def _rope_tables(pos):
    half = D_HEAD // 2
    inv = ROPE_BASE ** (-(jnp.arange(half, dtype=F32) / half))
    ang = pos.astype(F32)[:, None] * inv[None, :]
    cos, sin = jnp.cos(ang), jnp.sin(ang)
    reps = LANES // D_HEAD
    return (jnp.tile(jnp.concatenate([cos, cos], axis=1), (1, reps)),
            jnp.tile(jnp.concatenate([-sin, sin], axis=1), (1, reps)))


def _in_proj(x, norm_w, w_bf16, pos, tm, kv_position_minor=False):
    n, dm = x.shape
    cos, sin = _rope_tables(pos)
    pos_tiles = pos.shape[0] // tm
    row = lambda i: (i, 0)
    tab = lambda i: (i % pos_tiles, 0)
    f32_out = jax.ShapeDtypeStruct((n, D_RET), F32)
    bf_out = jax.ShapeDtypeStruct((n, D_RET), BF16)
    out_spec = pl.BlockSpec((tm, D_RET), row)
    kv_out, kv_spec = f32_out, out_spec
    if kv_position_minor:
        kv_out = jax.ShapeDtypeStruct((n // pos.shape[0], D_SB, pos.shape[0]), F32)
        kv_spec = pl.BlockSpec((None, D_SB, tm), lambda i: (i // pos_tiles, 0, i % pos_tiles))
    return pl.pallas_call(
        functools.partial(_in_proj_kernel, kv_position_minor=kv_position_minor),
        grid=(n // tm,),
        in_specs=[pl.BlockSpec((tm, dm), row), _const_spec((1, dm)), _const_spec(w_bf16.shape),
                  pl.BlockSpec((tm, LANES), tab), pl.BlockSpec((tm, LANES), tab)],
        out_specs=[out_spec] * 5 + [kv_spec] * 2 + [out_spec] * 2,
        out_shape=[bf_out, f32_out, bf_out, f32_out, bf_out, kv_out, kv_out, bf_out, bf_out],
        compiler_params=pltpu.CompilerParams(dimension_semantics=("parallel",), vmem_limit_bytes=VMEM_LIMIT),
        name="in_proj",
    )(x, norm_w.reshape(1, dm), w_bf16, cos, sin)


def _ret_tables(lg, c):
    idx = jnp.arange(c, dtype=F32)
    diff = idx[:, None] - idx[None, :]
    decay = jnp.where(diff >= 0, jnp.exp(lg[:, None, None] * jnp.maximum(diff, 0.0)), 0.0)
    cross = jnp.exp(lg[:, None] * (idx[None, :] + 1.0))
    kdec = jnp.exp(lg[:, None] * (c - 1.0 - idx[None, :]))
    sdec = jnp.exp(lg * c)
    return decay, cross, kdec, sdec


def _ret_prompt_kernel(q_ref, k_ref, v_ref, dec_ref, cross_ref, kdec_ref, sdec_ref, o_ref, s_ref, s_scr, *, chunks):
    step = pl.program_id(1)
    c = q_ref.shape[0] // chunks

    @pl.when(step == 0)
    def _():
        s_scr[...] = jnp.zeros_like(s_scr)

    lane = lax.broadcasted_iota(jnp.int32, (1, LANES), 1)
    first = lane < D_HEAD
    row_first = lax.broadcasted_iota(jnp.int32, (LANES, 1), 0) < D_HEAD
    same_head = row_first == first

    for p in range(PAIRS):
        sl = slice(p * LANES, (p + 1) * LANES)
        s = s_scr[p]
        for j in range(chunks):
            rows = slice(j * c, (j + 1) * c)
            q2 = q_ref[rows, sl]
            k2f = k_ref[rows, sl]
            v2 = v_ref[rows, sl]
            qk = _dot_nt(q2, _head_split(k2f.astype(BF16), first))
            prob = (qk * dec_ref[p]).astype(BF16)
            o = _dot(prob, _head_split(v2, first))
            o_ref[rows, sl] = o + _dot(q2, s.astype(BF16)) * cross_ref[:, sl]
            kd = (k2f * kdec_ref[:, sl]).astype(BF16)
            upd = _dot_tn(kd, v2)
            s = sdec_ref[p] * s + jnp.where(same_head, upd, 0.0)
        s_scr[p] = s

    @pl.when(step == pl.num_programs(1) - 1)
    def _():
        s_ref[...] = s_scr[...]


def _ret_prompt(rq, rk, rv, lg):
    b, s, _ = rq.shape
    c = RET_CHUNK
    decay, cross, kdec, sdec = _ret_tables(lg, c)
    dec_pairs = decay.reshape(PAIRS, 2, c, c).transpose(0, 2, 1, 3).reshape(PAIRS, c, 2 * c)
    lanes = lambda t: jnp.repeat(t.T, D_HEAD, axis=1)
    sdec_pairs = jnp.broadcast_to(jnp.repeat(sdec, D_HEAD).reshape(PAIRS, LANES, 1), (PAIRS, LANES, LANES))
    chunks = math.gcd(s // c, 4)
    blk = pl.BlockSpec((None, chunks * c, D_RET), lambda bi, ci: (bi, ci, 0))
    o, st = pl.pallas_call(
        functools.partial(_ret_prompt_kernel, chunks=chunks),
        grid=(b, s // (chunks * c)),
        in_specs=[blk, blk, blk, _const_spec((PAIRS, c, 2 * c)), _const_spec((c, D_RET)),
                  _const_spec((c, D_RET)), _const_spec((PAIRS, LANES, LANES))],
        out_specs=[blk, pl.BlockSpec((None, PAIRS, LANES, LANES), lambda bi, ci: (bi, 0, 0, 0))],
        out_shape=[jax.ShapeDtypeStruct((b, s, D_RET), F32),
                   jax.ShapeDtypeStruct((b, PAIRS, LANES, LANES), F32)],
        scratch_shapes=[pltpu.VMEM((PAIRS, LANES, LANES), F32)],
        compiler_params=pltpu.CompilerParams(dimension_semantics=("parallel", "arbitrary"),
                                             vmem_limit_bytes=VMEM_LIMIT),
        name="ret_prompt",
    )(rq, rk, rv, dec_pairs, lanes(cross), lanes(kdec), sdec_pairs)
    st = st.reshape(b, PAIRS, 2, D_HEAD, 2, D_HEAD)
    st = jnp.stack([st[:, :, 0, :, 0, :], st[:, :, 1, :, 1, :]], axis=2).reshape(b, H_RET, D_HEAD, D_HEAD)
    return o, st


def _suffix_matrix(n):
    j = np.arange(2 * n)[:, None] % n
    s = np.arange(2 * n)[None, :]
    return jnp.asarray(-np.where(s < n, j >= s, True).astype(np.float32), dtype=BF16)


def _sb_prompt_kernel(bias_ref, tri_ref, q_ref, k_ref, v_ref, o_ref, acc_ref, c_ref, qk_ref, p_ref, qs_ref, *,
                      tq, tk):
    i = pl.program_id(2)
    first = lax.broadcasted_iota(jnp.int32, (1, LANES), 1) < D_HEAD
    bias = bias_ref[...]
    tri = tri_ref[...]
    acc_ref[...] = jnp.zeros_like(acc_ref)
    c_ref[...] = jnp.zeros_like(c_ref)
    diag_blocks = tq // tk
    qs_ref[...] = jnp.concatenate(_split2(q_ref[...].astype(F32) * LOG2E), axis=1)

    def scores(kb, row0):
        kz = _head_split(k_ref[pl.ds(pl.multiple_of(kb * tk, tk), tk), :], first)
        return _dot_nt(qs_ref[row0:, :], jnp.concatenate([kz, kz], axis=1))

    def weights(qk, row0, diagonal):
        rows = slice(row0, tq)
        zl = qk + bias
        u = _neg_log2_survival(zl)
        if diagonal:
            qi = lax.broadcasted_iota(jnp.int32, (tq - row0, tk), 0)
            ki = lax.broadcasted_iota(jnp.int32, (tq - row0, tk), 1)
            keep = ki < qi
            u = jnp.where(jnp.concatenate([keep, keep], axis=1), u, 0.0)
        hi, lo = _split2(u)
        probs = []
        for h in range(2):
            sl = slice(h * tk, (h + 1) * tk)
            r = _dot(jnp.concatenate([hi[:, sl], lo[:, sl]], axis=1), tri)
            a = jnp.exp2(zl[:, sl] + r[:, :tk] + c_ref[h, rows, :])
            if diagonal:
                a = jnp.where(keep, a, 0.0)
            probs.append(a.astype(BF16))
            c_ref[h, rows, :] += r[:, tk:]
        return jnp.concatenate(probs, axis=1)

    def accumulate(kb, probs, row0):
        v2 = v_ref[pl.ds(pl.multiple_of(kb * tk, tk), tk), :]
        acc_ref[row0:, :] += _dot(probs, _head_split(v2, first))

    n_bulk = i * diag_blocks
    qk = scores(n_bulk + diag_blocks - 1, (diag_blocks - 1) * tk)
    probs = None
    for d in reversed(range(diag_blocks)):
        qk_next = scores(n_bulk + d - 1, (d - 1) * tk) if d else scores(jnp.maximum(n_bulk - 1, 0), 0)
        if probs is not None:
            accumulate(n_bulk + d + 1, probs, (d + 1) * tk)
        probs = weights(qk, d * tk, True)
        qk = qk_next
    qk_ref[0] = qk
    p_ref[1] = probs

    def walk(first_kb, blocks):
        for j in range(blocks):
            cur = j % 2
            qk_ref[1 - cur] = scores(jnp.maximum(first_kb - j - 1, 0), 0)
            accumulate(first_kb - j + 1, p_ref[1 - cur], 0)
            p_ref[cur] = weights(qk_ref[cur], 0, False)

    long_trip = 2 * diag_blocks

    def long_body(step, carry):
        walk(n_bulk - 1 - long_trip * step, long_trip)
        return carry

    def short_body(step, carry):
        walk(n_bulk - 1 - long_trip * (i // 2), diag_blocks)
        return carry

    lax.fori_loop(0, i // 2, long_body, 0)
    lax.fori_loop(0, i % 2, short_body, 0)
    accumulate(0, p_ref[1], 0)
    o_ref[...] = acc_ref[...].astype(o_ref.dtype)


def _sb_prompt(sq, skb, svb, bias, tq=512, tk=128):
    b, s, _ = sq.shape
    tq = math.gcd(s, tq)
    assert tq % (2 * tk) == 0, "the ping-pong pipeline walks key blocks in pairs"
    bias_pairs = jnp.repeat(bias.astype(F32) * LOG2E, tk).reshape(PAIRS, 1, 2 * tk)
    kv = pl.BlockSpec((None, s, LANES), lambda bi, p, i: (bi, 0, p))
    qo = pl.BlockSpec((None, tq, LANES), lambda bi, p, i: (bi, i, p))
    return pl.pallas_call(
        functools.partial(_sb_prompt_kernel, tq=tq, tk=tk),
        grid=(b, PAIRS, s // tq),
        in_specs=[pl.BlockSpec((None, 1, 2 * tk), lambda bi, p, i: (p, 0, 0)),
                  _const_spec((2 * tk, 2 * tk)), qo, kv, kv],
        out_specs=qo,
        out_shape=jax.ShapeDtypeStruct((b, s, D_SB), BF16),
        scratch_shapes=[pltpu.VMEM((tq, LANES), F32), pltpu.VMEM((2, tq, tk), F32),
                        pltpu.VMEM((2, tq, 2 * tk), F32), pltpu.VMEM((2, tq, 2 * tk), BF16),
                        pltpu.VMEM((tq, 2 * LANES), BF16)],
        compiler_params=pltpu.CompilerParams(dimension_semantics=("parallel", "parallel", "arbitrary"),
                                             vmem_limit_bytes=VMEM_LIMIT),
        name="sb_prompt",
    )(bias_pairs, _suffix_matrix(tk), sq, skb, svb)


def _ret_sample_kernel(q_ref, k_ref, kt_ref, v_ref, s_ref, dec_ref, cross_ref, kdec_ref, sdec_ref, o_ref, so_ref):
    q = q_ref[...]
    v = v_ref[...]
    s = s_ref[...]
    bdot = lambda a, b_, dims: lax.dot_general(a, b_, (dims, ((0,), (0,))), preferred_element_type=F32)
    qk = bdot(q, k_ref[...].astype(BF16), ((2,), (2,)))
    prob = (qk * dec_ref[...]).astype(BF16)
    o = bdot(prob, v, ((2,), (1,)))
    o = o + bdot(q, s.astype(BF16), ((2,), (1,))) * cross_ref[...]
    o_ref[...] = o
    kd = (kt_ref[...] * kdec_ref[...]).astype(BF16)
    so_ref[...] = sdec_ref[...] * s + bdot(kd, v, ((2,), (1,)))


def _ret_sample(rq, rk, rv, state, lg, seq_blk=8, t_pad=16):
    bs = state.shape[0]
    t = rq.shape[0] // bs
    n = bs * H_RET
    nb = seq_blk * H_RET

    def heads(x):
        x = x.reshape(bs, t, H_RET, D_HEAD).transpose(0, 2, 1, 3).reshape(n, t, D_HEAD)
        return jnp.pad(x, ((0, 0), (0, t_pad - t), (0, 0)))

    decay, cross, kdec, sdec = _ret_tables(lg, t)
    pad_t = lambda x, axes: jnp.pad(x, [(0, t_pad - t) if a in axes else (0, 0) for a in range(x.ndim)])
    per_blk = lambda x, shape: jnp.tile(jnp.broadcast_to(x, (H_RET,) + shape), (seq_blk, 1, 1))
    dec_t = per_blk(pad_t(decay, (1, 2)), (t_pad, t_pad))
    cross_t = per_blk(pad_t(cross, (1,))[:, :, None], (t_pad, D_HEAD))
    kdec_t = per_blk(pad_t(kdec, (1,))[:, None, :], (D_HEAD, t_pad))
    sdec_t = per_blk(sdec[:, None, None], (D_HEAD, D_HEAD))
    k_heads = heads(rk)
    qkv = pl.BlockSpec((nb, t_pad, D_HEAD), lambda i: (i, 0, 0))
    ktr = pl.BlockSpec((nb, D_HEAD, t_pad), lambda i: (i, 0, 0))
    st = pl.BlockSpec((nb, D_HEAD, D_HEAD), lambda i: (i, 0, 0))
    o, s_new = pl.pallas_call(
        _ret_sample_kernel,
        grid=(bs // seq_blk,),
        in_specs=[qkv, qkv, ktr, qkv, st, _const_spec(dec_t.shape), _const_spec(cross_t.shape),
                  _const_spec(kdec_t.shape), _const_spec(sdec_t.shape)],
        out_specs=[qkv, st],
        out_shape=[jax.ShapeDtypeStruct((n, t_pad, D_HEAD), F32),
                   jax.ShapeDtypeStruct((n, D_HEAD, D_HEAD), F32)],
        compiler_params=pltpu.CompilerParams(dimension_semantics=("parallel",), vmem_limit_bytes=VMEM_LIMIT),
        name="ret_sample",
    )(heads(rq), k_heads, k_heads.transpose(0, 2, 1), heads(rv), state.reshape(n, D_HEAD, D_HEAD),
      dec_t, cross_t, kdec_t, sdec_t)
    o = o[:, :t].reshape(bs, H_RET, t, D_HEAD).transpose(0, 2, 1, 3).reshape(bs * t, D_RET)
    return o, s_new.reshape(bs, H_RET, D_HEAD, D_HEAD)


def _sb_sample_kernel(pt_ref, qbd_ref, bias_ref, tri_ref, kn_ref, vn_ref, *rest, pages_per_step, page, t):
    kp = rest[:pages_per_step]
    vp = rest[pages_per_step:2 * pages_per_step]
    o_ref, acc_ref, c_ref = rest[2 * pages_per_step:]
    step = pl.program_id(1)
    qbd = qbd_ref[...]
    bias = bias_ref[...]
    tri = tri_ref[...]
    nq = qbd.shape[0]

    def attend(qk, keep):
        n = qk.shape[1] // page
        blk = lambda x, g: x[:, g * page:(g + 1) * page]
        zl = qk * LOG2E + jnp.concatenate([bias] * n, axis=1)
        u = _neg_log2_survival(zl)
        if keep is not None:
            u = jnp.where(keep, u, 0.0)
        hi, lo = _split2(u)
        lhs = jnp.concatenate([jnp.concatenate([blk(hi, g), blk(lo, g)], axis=1) for g in range(n)], axis=0)
        r = _dot(lhs, tri)
        c = c_ref[...]
        probs = []
        for g in range(n):
            rg = r[g * nq:(g + 1) * nq]
            a = jnp.exp2(blk(zl, g) + rg[:, :page] + c)
            if keep is not None:
                a = jnp.where(keep, a, 0.0)
            probs.append(a.astype(BF16))
            c = c + rg[:, page:]
        c_ref[...] = c
        return jnp.concatenate(probs, axis=1)

    @pl.when(step == 0)
    def _():
        acc_ref[...] = jnp.zeros_like(acc_ref)
        c_ref[...] = jnp.zeros_like(c_ref)
        pad = jnp.zeros((page - kn_ref.shape[0], D_SB), F32)
        kn = jnp.concatenate([kn_ref[...], pad], axis=0).astype(BF16)
        vn = jnp.concatenate([vn_ref[...], pad], axis=0).astype(BF16)
        qt = lax.broadcasted_iota(jnp.int32, (nq, page), 0) // H_SB
        key = lax.broadcasted_iota(jnp.int32, (nq, page), 1)
        acc_ref[...] += _dot(attend(_dot_nt(qbd, kn), key < qt), vn)

    kcat = jnp.concatenate([kp[g][...].astype(BF16) for g in range(pages_per_step)], axis=1)
    vcat = jnp.concatenate([vp[g][...].astype(BF16) for g in range(pages_per_step)], axis=1)
    acc_ref[...] += _dot_nt(attend(_dot(qbd, kcat), None), vcat)

    @pl.when(step == pl.num_programs(1) - 1)
    def _():
        acc = acc_ref[...]
        row_head = lax.broadcasted_iota(jnp.int32, acc.shape, 0) % H_SB
        col_head = lax.broadcasted_iota(jnp.int32, acc.shape, 1) // D_HEAD
        own = jnp.where(row_head == col_head, acc, 0.0)
        o_ref[...] = jnp.sum(own.reshape(t, H_SB, D_SB), axis=1).astype(o_ref.dtype)


def _sb_sample(sq, sk, sv, cache_k, cache_v, page_table, bias, pages_per_step=32, t_pad=8):
    bs, n_pages = page_table.shape
    pages_per_step = math.gcd(n_pages, pages_per_step)
    t = sq.shape[0] // bs
    n_phys, page = cache_k.shape[:2]
    nq = t * H_SB
    eye = jnp.eye(H_SB, dtype=sq.dtype)
    qbd = (sq.reshape(bs, t, 1, H_SB, D_HEAD) * eye[None, None, :, :, None]).reshape(bs, nq, D_SB)
    bias_rows = jnp.broadcast_to(jnp.tile(bias.astype(F32) * LOG2E, t)[:, None], (nq, page))
    new = lambda x: jnp.pad(x.reshape(bs, t, D_SB), ((0, 0), (0, t_pad - t), (0, 0)))
    n_steps = n_pages // pages_per_step

    def page_spec(g):
        return pl.BlockSpec((None, D_SB, page),
                            lambda b, s, pt: (pt[b, n_pages - 1 - (s * pages_per_step + g)], 0, 0))

    per_seq = lambda shape: pl.BlockSpec((None,) + shape, lambda b, s, pt: (b, 0, 0))
    const = lambda shape: pl.BlockSpec(shape, lambda b, s, pt: (0,) * len(shape))
    ck = cache_k.transpose(0, 2, 3, 1).reshape(n_phys, D_SB, page)
    cv = cache_v.transpose(0, 2, 3, 1).reshape(n_phys, D_SB, page)
    grid_spec = pltpu.PrefetchScalarGridSpec(
        num_scalar_prefetch=1,
        grid=(bs, n_steps),
        in_specs=[per_seq((nq, D_SB)), const((nq, page)), const((2 * page, 2 * page)),
                  per_seq((t_pad, D_SB)), per_seq((t_pad, D_SB))]
                 + [page_spec(g) for g in range(pages_per_step)] * 2,
        out_specs=per_seq((t, D_SB)),
        scratch_shapes=[pltpu.VMEM((nq, D_SB), F32), pltpu.VMEM((nq, page), F32)],
    )
    o = pl.pallas_call(
        functools.partial(_sb_sample_kernel, pages_per_step=pages_per_step, page=page, t=t),
        grid_spec=grid_spec,
        out_shape=jax.ShapeDtypeStruct((bs, t, D_SB), BF16),
        compiler_params=pltpu.CompilerParams(dimension_semantics=("parallel", "arbitrary"),
                                             vmem_limit_bytes=VMEM_LIMIT),
        name="sb_sample",
    )(page_table, qbd, bias_rows, _suffix_matrix(page), new(sk), new(sv),
      *([ck] * pages_per_step), *([cv] * pages_per_step))
    return o.reshape(bs * t, D_SB)


def _merge_ffn_kernel(x_ref, oret_ref, g_ref, osb_ref, gm_ref, rnw_ref, wo_ref, n2w_ref, wg_ref, wu_ref, wd_ref,
                      fnw_ref, y_ref, *, ff_chunks):
    gm = gm_ref[...]

    def head_mean(v):
        hi, mid, lo = _split3(v)
        return _dot(hi, gm) + _dot(mid, gm) + _dot(lo, gm)

    o = oret_ref[...]
    d = o - head_mean(o)
    n = d * lax.rsqrt(head_mean(d * d) + EPS) * rnw_ref[...]
    g = g_ref[...]
    r = g * (1.0 / (1.0 + jnp.exp(-g))) * n
    x = x_ref[...]
    x = x + _dot(r.astype(BF16), wo_ref[:D_RET, :]) + _dot(osb_ref[...], wo_ref[D_RET:, :])
    h = _rms_scale(x, n2w_ref[...]).astype(BF16)
    d_ff = wg_ref.shape[1]
    fc = d_ff // ff_chunks
    down = None
    for c in range(ff_chunks):
        sl = slice(c * fc, (c + 1) * fc)
        gate = _dot(h, wg_ref[:, sl])
        act = gate * (1.0 / (1.0 + jnp.exp(-gate))) * _dot(h, wu_ref[:, sl])
        part = _dot(act.astype(BF16), wd_ref[sl, :])
        down = part if down is None else down + part
    x = x + down
    y_ref[...] = _rms_scale(x, fnw_ref[...])


def _merge_ffn(x, o_ret, g, o_sb, ret_norm_w, w_out, norm2_w, w_gate, w_up, w_down, final_norm_w, tm):
    n, dm = x.shape
    d_ff = w_gate.shape[1]
    head = np.arange(D_RET) // D_HEAD
    gm = jnp.asarray((head[:, None] == head[None, :]) / D_HEAD, dtype=BF16)
    row = lambda width: pl.BlockSpec((tm, width), lambda i: (i, 0))
    vec = lambda w: w.reshape(1, -1).astype(F32)
    return pl.pallas_call(
        functools.partial(_merge_ffn_kernel, ff_chunks=2),
        grid=(n // tm,),
        in_specs=[row(dm), row(D_RET), row(D_RET), row(D_SB), _const_spec((D_RET, D_RET)), _const_spec((1, D_RET)),
                  _const_spec(w_out.shape), _const_spec((1, dm)), _const_spec(w_gate.shape),
                  _const_spec(w_up.shape), _const_spec(w_down.shape), _const_spec((1, dm))],
        out_specs=row(dm),
        out_shape=jax.ShapeDtypeStruct((n, dm), F32),
        compiler_params=pltpu.CompilerParams(dimension_semantics=("parallel",), vmem_limit_bytes=VMEM_LIMIT),
        name="merge_ffn",
    )(x, o_ret, g, o_sb, gm, vec(ret_norm_w), w_out, vec(norm2_w), w_gate, w_up, w_down, vec(final_norm_w))


def _row_tile(n, want):
    return math.gcd(n, want)


def kernel(x_prompt, x_sample, cache_k, cache_v, state_ret, page_table, norm1_w, w_in, sb_bias, ret_norm_w, w_out,
           norm2_w, w_gate, w_up, w_down, final_norm_w):
    depth = w_in.shape[0]
    assert depth == 1, "single-layer step"
    bp, s, dm = x_prompt.shape
    bs, t, _ = x_sample.shape
    n_pages = page_table.shape[1]
    past_len = n_pages * cache_k.shape[2]
    lg = jnp.log1p(-jnp.exp2(-5.0 - jnp.arange(H_RET, dtype=F32)))
    l = 0
    w_in_b, w_out_b = w_in[l].astype(BF16), w_out[l].astype(BF16)
    w_gate_b, w_up_b, w_down_b = w_gate[l].astype(BF16), w_up[l].astype(BF16), w_down[l].astype(BF16)
    tail = (ret_norm_w[l], w_out_b, norm2_w[l], w_gate_b, w_up_b, w_down_b, final_norm_w)

    xp = x_prompt.reshape(bp * s, dm)
    tm = _row_tile(s, 512)
    rq, rk, rv, rg, sq, sk_t, sv_t, skb, svb = _in_proj(xp, norm1_w[l], w_in_b, jnp.arange(s), tm,
                                                        kv_position_minor=True)
    seq = lambda a: a.reshape(bp, s, -1)
    o_ret, s_p = _ret_prompt(seq(rq), seq(rk), seq(rv), lg)
    o_sb = _sb_prompt(seq(sq), seq(skb), seq(svb), sb_bias[l])
    y_prompt = _merge_ffn(xp, o_ret.reshape(bp * s, D_RET), rg, o_sb.reshape(bp * s, D_SB), *tail, tm=tm)

    xs = x_sample.reshape(bs * t, dm)
    tms = _row_tile(bs * t, 512)
    pos_s = jnp.tile(past_len + jnp.arange(t), tms // t)
    rq_s, rk_s, rv_s, rg_s, sq_s, sk_s, sv_s, _, _ = _in_proj(xs, norm1_w[l], w_in_b, pos_s, tms)
    o_ret_s, s_s = _ret_sample(rq_s, rk_s, rv_s, state_ret[l].astype(F32), lg)
    o_sb_s = _sb_sample(sq_s, sk_s, sv_s, cache_k[l], cache_v[l], page_table, sb_bias[l])
    y_sample = _merge_ffn(xs, o_ret_s, rg_s, o_sb_s, *tail, tm=tms)

    kv = lambda a, b_, n: a.reshape(1, b_, n, H_SB, D_HEAD)
    kv_t = lambda a: a.reshape(1, bp, H_SB, D_HEAD, s).transpose(0, 1, 4, 2, 3)
    return (y_prompt.reshape(bp, s, dm), y_sample.reshape(bs, t, dm),
            kv_t(sk_t).astype(cache_k.dtype), kv_t(sv_t).astype(cache_v.dtype),
            s_p[None].astype(state_ret.dtype),
            kv(sk_s, bs, t).astype(cache_k.dtype), kv(sv_s, bs, t).astype(cache_v.dtype),
            s_s[None].astype(state_ret.dtype))
```
